```python
import math
import jax, jax.numpy as jnp
from jax import lax
import numpy as np

D_MODEL = 4096
BATCH = 4
SEQ = 2048
DEPTH = 2

CHUNK = 64
EPS = 1e-6
D_HGRN = 1024
HGRN_HEADS = 8
HGRN_HEAD_DIM = D_HGRN // HGRN_HEADS
HGRN_BLOCK = 16
D_SSM = 1024
SSM_GROUP = 16
SSM_GROUPS = D_SSM // SSM_GROUP
SSM_STATE = 64
DT_MIN = 0.001
DT_MAX = 0.1
D_ATT = D_MODEL - D_HGRN - D_SSM
ATT_HEADS = 16
ATT_HEAD_DIM = D_ATT // ATT_HEADS
LEFT_CHUNKS = 8
BAND = (LEFT_CHUNKS + 1) * CHUNK
MAX_REL = 256
D_IN = 4 * D_HGRN + D_SSM + 3 * D_ATT
SPLITS = (D_HGRN, 2 * D_HGRN, 3 * D_HGRN, 4 * D_HGRN, 4 * D_HGRN + D_SSM,
          4 * D_HGRN + D_SSM + D_ATT, 4 * D_HGRN + D_SSM + 2 * D_ATT)
PEER_HEADS = 8
PEER_KEYS = 128
PEER_EXPERTS = PEER_KEYS * PEER_KEYS
PEER_QDIM = 256
PEER_TOPK = 16
PEER_BLOCK = 128

kernel_name = "hymba_hgrn2_s5_chunkattn_peer"


def rmsnorm(x, g):
    xf = x.astype(jnp.float32)
    y = xf * lax.rsqrt(jnp.mean(xf * xf, axis=-1, keepdims=True) + EPS)
    return (y * g.astype(jnp.float32)).astype(x.dtype)


def hgrn2_mixer(q, f_logit, i, g, lb, norm_g):
    f32 = jnp.float32
    bsz, seq, _ = q.shape
    nb = seq // HGRN_BLOCK
    shp = (bsz, nb, HGRN_BLOCK, HGRN_HEADS, HGRN_HEAD_DIM)
    z = f_logit.astype(f32)
    lbf = lb.astype(f32)
    log_f = jnp.logaddexp(jnp.log(lbf), jnp.log1p(-lbf) + jax.nn.log_sigmoid(z)).reshape(shp)
    k = ((1.0 - lbf) * jax.nn.sigmoid(-z)).reshape(shp)
    qb = q.astype(f32).reshape(shp)
    vb = i.astype(f32).reshape(shp)
    b = jnp.cumsum(log_f, axis=2)
    causal = jnp.tril(jnp.ones((HGRN_BLOCK, HGRN_BLOCK), dtype=bool))
    diff = b[:, :, :, None] - b[:, :, None, :]
    decay = jnp.exp(jnp.where(causal[None, None, :, :, None, None], diff, -jnp.inf))
    scores = jnp.einsum('bntshd,bnthd,bnshd->bnhts', decay, qb, k)
    o_intra = jnp.einsum('bnhts,bnshv->bnthv', scores, vb)
    b_last = b[:, :, -1]
    d_state = jnp.einsum('bnshk,bnshv->bnhkv', k * jnp.exp(b_last[:, :, None] - b), vb)

    def step(state, inp):
        dec, ds = inp
        return dec[..., None] * state + ds, state

    init = jnp.zeros((bsz, HGRN_HEADS, HGRN_HEAD_DIM, HGRN_HEAD_DIM), f32)
    _, starts = lax.scan(step, init, (jnp.moveaxis(jnp.exp(b_last), 1, 0), jnp.moveaxis(d_state, 1, 0)))
    starts = jnp.moveaxis(starts, 0, 1)
    o_inter = jnp.einsum('bnthk,bnhkv->bnthv', qb * jnp.exp(b), starts)
    o = (o_intra + o_inter).reshape(bsz, seq, HGRN_HEADS, HGRN_HEAD_DIM)
    o = o * lax.rsqrt(jnp.mean(o * o, axis=-1, keepdims=True) + EPS)
    o = o * norm_g.astype(f32).reshape(HGRN_HEADS, HGRN_HEAD_DIM)
    o = o.reshape(bsz, seq, D_HGRN) * jax.nn.silu(g.astype(f32))
    return o.astype(q.dtype)


def s5_mixer(u, a_re, a_im, log_dt, b_re, b_im, c_re, c_im, d_skip, w_glu, norm_g):
    f32 = jnp.float32
    bsz, seq, _ = u.shape
    uf = u.astype(f32).reshape(bsz, seq, SSM_GROUPS, SSM_GROUP)
    a = lax.complex(a_re.astype(f32), a_im.astype(f32))
    dt = jnp.exp(log_dt.astype(f32))[:, None]
    a_bar = jnp.exp(a * dt)
    b_mat = lax.complex(b_re.astype(f32), b_im.astype(f32))
    b_bar = ((a_bar - 1.0) / a)[..., None] * b_mat
    c_mat = lax.complex(c_re.astype(f32), c_im.astype(f32))
    bu = jnp.einsum('bsgm,gpm->bsgp', uf.astype(jnp.complex64), b_bar)
    a_elems = jnp.broadcast_to(a_bar, bu.shape)

    def combine(left, right):
        a1, x1 = left
        a2, x2 = right
        return a1 * a2, a2 * x1 + x2

    _, states = lax.associative_scan(combine, (a_elems, bu), axis=1)
    y = jnp.einsum('bsgp,gmp->bsgm', states, c_mat).real
    y = y + d_skip.astype(f32).reshape(SSM_GROUPS, SSM_GROUP) * uf
    y = jax.nn.gelu(y.reshape(bsz, seq, D_SSM))
    y = y * jax.nn.sigmoid(y @ w_glu.astype(f32))
    return rmsnorm(y, norm_g).astype(u.dtype)


def chunk_attention(q, k, v, rel_bias, norm_g):
    bsz, seq, _ = q.shape
    n = seq // CHUNK
    hd = (ATT_HEADS, ATT_HEAD_DIM)
    qc = q.reshape(bsz, n, CHUNK, *hd)
    pad = ((0, 0), (LEFT_CHUNKS * CHUNK, 0), (0, 0))
    kp = jnp.pad(k, pad).reshape(bsz, n + LEFT_CHUNKS, CHUNK, *hd)
    vp = jnp.pad(v, pad).reshape(bsz, n + LEFT_CHUNKS, CHUNK, *hd)
    band_idx = jnp.arange(n)[:, None] + jnp.arange(LEFT_CHUNKS + 1)[None, :]
    kband = kp[:, band_idx].reshape(bsz, n, BAND, *hd)
    vband = vp[:, band_idx].reshape(bsz, n, BAND, *hd)
    scores = jnp.einsum('bnqhd,bnkhd->bnhqk', qc, kband).astype(jnp.float32) * (ATT_HEAD_DIM ** -0.5)
    q_pos = jnp.arange(CHUNK)[:, None] + LEFT_CHUNKS * CHUNK
    k_pos = jnp.arange(BAND)[None, :]
    rel_idx = jnp.clip(q_pos - k_pos, -MAX_REL, MAX_REL) + MAX_REL
    bias = rel_bias.astype(jnp.float32)[:, rel_idx]
    key_chunk = jnp.arange(n)[:, None] - LEFT_CHUNKS + (jnp.arange(BAND) // CHUNK)[None, :]
    valid = key_chunk >= 0
    scores = jnp.where(valid[None, :, None, None, :], scores + bias[None, None], -1e30)
    p = jax.nn.softmax(scores, axis=-1).astype(v.dtype)
    o = jnp.einsum('bnhqk,bnkhd->bnqhd', p, vband).reshape(bsz, seq, D_ATT)
    return rmsnorm(o, norm_g)


def peer_ffn(x, w_q, keys1, keys2, expert_u, expert_v):
    bsz, seq, d = x.shape
    t = bsz * seq
    xt = x.reshape(t, d)
    q = (xt @ w_q).reshape(t, PEER_HEADS, 2, PEER_QDIM // 2)
    s1 = jnp.einsum('thc,kc->thk', q[:, :, 0], keys1).astype(jnp.float32)
    s2 = jnp.einsum('thc,kc->thk', q[:, :, 1], keys2).astype(jnp.float32)
    v1, i1 = lax.top_k(s1, PEER_TOPK)
    v2, i2 = lax.top_k(s2, PEER_TOPK)
    cand = (v1[..., :, None] + v2[..., None, :]).reshape(t, PEER_HEADS, PEER_TOPK * PEER_TOPK)
    cand_idx = (i1[..., :, None] * PEER_KEYS + i2[..., None, :]).reshape(t, PEER_HEADS, PEER_TOPK * PEER_TOPK)
    top_s, pos = lax.top_k(cand, PEER_TOPK)
    experts = jnp.take_along_axis(cand_idx, pos, axis=-1)
    gates = jax.nn.softmax(top_s, axis=-1)
    nb = t // PEER_BLOCK
    hk = PEER_HEADS * PEER_TOPK

    def block_fn(args):
        xb, eb, gb = args
        u_sel = jnp.take(expert_u, eb, axis=0)
        h = jax.nn.gelu(jnp.einsum('td,tkd->tk', xb, u_sel).astype(jnp.float32))
        w = (gb * h).astype(xb.dtype)
        v_sel = jnp.take(expert_v, eb, axis=0)
        return jnp.einsum('tk,tkd->td', w, v_sel)

    out = lax.map(block_fn, (xt.reshape(nb, PEER_BLOCK, d),
                             experts.reshape(nb, PEER_BLOCK, hk),
                             gates.reshape(nb, PEER_BLOCK, hk)))
    return out.reshape(bsz, seq, d).astype(x.dtype)


def setup_inputs(seed: int = 0) -> dict:
    key = jax.random.key(seed)
    ks = jax.random.split(key, 26)
    f32 = jnp.float32

    def nrm(k, shape, std):
        return std * jax.random.normal(k, shape, f32)

    g, p, m = SSM_GROUPS, SSM_STATE, SSM_GROUP
    n_idx = jnp.arange(p, dtype=f32)
    return {
        'x': nrm(ks[0], (BATCH, SEQ, D_MODEL), 1.0),
        'w_in': nrm(ks[1], (DEPTH, D_MODEL, D_IN), D_MODEL ** -0.5),
        'w_out': nrm(ks[2], (DEPTH, D_MODEL, D_MODEL), D_MODEL ** -0.5),
        'hgrn_lb': nrm(ks[3], (DEPTH, D_HGRN), 1.0),
        'hgrn_norm': 1.0 + nrm(ks[4], (DEPTH, D_HGRN), 0.01),
        'ssm_a_re': -0.5 + nrm(ks[5], (DEPTH, g, p), 0.01),
        'ssm_a_im': math.pi * n_idx + nrm(ks[6], (DEPTH, g, p), 0.01),
        'ssm_log_dt': jax.random.uniform(ks[7], (DEPTH, g), f32, math.log(DT_MIN), math.log(DT_MAX)),
        'ssm_b_re': nrm(ks[8], (DEPTH, g, p, m), (2 * m) ** -0.5),
        'ssm_b_im': nrm(ks[9], (DEPTH, g, p, m), (2 * m) ** -0.5),
        'ssm_c_re': nrm(ks[10], (DEPTH, g, m, p), (2 * p) ** -0.5),
        'ssm_c_im': nrm(ks[11], (DEPTH, g, m, p), (2 * p) ** -0.5),
        'ssm_d': nrm(ks[12], (DEPTH, D_SSM), 1.0),
        'ssm_w_glu': nrm(ks[13], (DEPTH, D_SSM, D_SSM), D_SSM ** -0.5),
        'ssm_norm': 1.0 + nrm(ks[14], (DEPTH, D_SSM), 0.01),
        'att_rel_bias': nrm(ks[15], (DEPTH, ATT_HEADS, 2 * MAX_REL + 1), 0.1),
        'att_norm': 1.0 + nrm(ks[16], (DEPTH, D_ATT), 0.01),
        'norm_mix': 1.0 + nrm(ks[17], (DEPTH, D_MODEL), 0.01),
        'norm_ffn': 1.0 + nrm(ks[18], (DEPTH, D_MODEL), 0.01),
        'peer_w_q': nrm(ks[19], (DEPTH, D_MODEL, PEER_HEADS * PEER_QDIM), D_MODEL ** -0.5),
        'peer_keys1': nrm(ks[20], (DEPTH, PEER_KEYS, PEER_QDIM // 2), (PEER_QDIM // 2) ** -0.5),
        'peer_keys2': nrm(ks[21], (DEPTH, PEER_KEYS, PEER_QDIM // 2), (PEER_QDIM // 2) ** -0.5),
        'peer_u': nrm(ks[22], (DEPTH, PEER_EXPERTS, D_MODEL), D_MODEL ** -0.5),
        'peer_v': nrm(ks[23], (DEPTH, PEER_EXPERTS, D_MODEL), PEER_HEADS ** -0.5),
        'norm_final': 1.0 + nrm(ks[24], (D_MODEL,), 0.01),
    }


def reference(x, w_in, w_out, hgrn_lb, hgrn_norm, ssm_a_re, ssm_a_im, ssm_log_dt,
              ssm_b_re, ssm_b_im, ssm_c_re, ssm_c_im, ssm_d, ssm_w_glu, ssm_norm,
              att_rel_bias, att_norm, norm_mix, norm_ffn, peer_w_q, peer_keys1,
              peer_keys2, peer_u, peer_v, norm_final):
    lb_all = jnp.cumsum(jax.nn.softmax(hgrn_lb.astype(jnp.float32), axis=0), axis=0)
    lb_all = lb_all - lb_all[0]
    h = x
    for layer in range(DEPTH):
        xn = rmsnorm(h, norm_mix[layer])
        proj = xn @ w_in[layer]
        q_a, f_a, i_a, g_a, u_b, q_c, k_c, v_c = jnp.split(proj, SPLITS, axis=-1)
        y_a = hgrn2_mixer(q_a, f_a, i_a, g_a, lb_all[layer], hgrn_norm[layer])
        y_b = s5_mixer(u_b, ssm_a_re[layer], ssm_a_im[layer], ssm_log_dt[layer],
                       ssm_b_re[layer], ssm_b_im[layer], ssm_c_re[layer], ssm_c_im[layer],
                       ssm_d[layer], ssm_w_glu[layer], ssm_norm[layer])
        y_c = chunk_attention(q_c, k_c, v_c, att_rel_bias[layer], att_norm[layer])
        mixed = jnp.concatenate([y_a.astype(xn.dtype), y_b.astype(xn.dtype), y_c.astype(xn.dtype)], axis=-1)
        h = h + (mixed @ w_out[layer]).astype(h.dtype)
        hn = rmsnorm(h, norm_ffn[layer])
        h = h + peer_ffn(hn, peer_w_q[layer], peer_keys1[layer], peer_keys2[layer],
                         peer_u[layer], peer_v[layer]).astype(h.dtype)
    return rmsnorm(h, norm_final)
```

```python
import functools
import math

import jax
import jax.numpy as jnp
import numpy as np
from jax import lax
from jax.experimental import pallas as pl
from jax.experimental.pallas import tpu as pltpu

F32 = jnp.float32
BF16 = jnp.bfloat16

EPS = 1e-6
DEPTH = 2
CHUNK = 64
D_HGRN = 1024
HGRN_HEADS = 8
HGRN_HEAD_DIM = D_HGRN // HGRN_HEADS
HGRN_TILE = 128
D_SSM = 1024
SSM_GROUP = 16
SSM_GROUPS = D_SSM // SSM_GROUP
SSM_STATE = 64
SSM_CHUNK = 16
D_ATT = 2048
ATT_HEADS = 16
ATT_HEAD_DIM = D_ATT // ATT_HEADS
LEFT_CHUNKS = 8
MAX_REL = 256
ATT_TQ = 256
ATT_WIN = ATT_TQ + LEFT_CHUNKS * CHUNK
PEER_HEADS = 8
PEER_KEYS = 128
PEER_QDIM = 256
PEER_TOPK = 16

VMEM_LIMIT = 56 * 1024 * 1024

NT_DIMS = (((1,), (1,)), ((), ()))
TN_DIMS = (((0,), (0,)), ((), ()))


def _params(*sem):
    return pltpu.CompilerParams(dimension_semantics=sem, vmem_limit_bytes=VMEM_LIMIT)


def _rms(x, g):
    ms = jnp.mean(x * x, axis=-1, keepdims=True)
    return x * lax.rsqrt(ms + EPS) * g


def _rmsnorm_body(x_ref, g_ref, o_ref):
    o_ref[...] = _rms(x_ref[...], g_ref[...]).astype(o_ref.dtype)


def _add_rmsnorm_body(a_ref, b_ref, g_ref, *out_refs):
    s = a_ref[...] + b_ref[...]
    if len(out_refs) == 2:
        out_refs[0][...] = s
    out_refs[-1][...] = _rms(s, g_ref[...]).astype(out_refs[-1].dtype)


def rmsnorm_call(x, g, out_dtype, tm=256):
    t, d = x.shape
    row = pl.BlockSpec((tm, d), lambda i: (i, 0))
    return pl.pallas_call(
        _rmsnorm_body,
        grid=(t // tm,),
        in_specs=[row, pl.BlockSpec((1, d), lambda i: (0, 0))],
        out_specs=row,
        out_shape=jax.ShapeDtypeStruct((t, d), out_dtype),
        compiler_params=_params("parallel"),
        name="rmsnorm",
    )(x, g.reshape(1, d).astype(F32))


def add_rmsnorm_call(a, b, g, out_dtype, keep_sum, tm=256):
    t, d = a.shape
    row = pl.BlockSpec((tm, d), lambda i: (i, 0))
    out_specs = [row, row] if keep_sum else row
    normed = jax.ShapeDtypeStruct((t, d), out_dtype)
    out_shape = [jax.ShapeDtypeStruct((t, d), F32), normed] if keep_sum else normed
    return pl.pallas_call(
        _add_rmsnorm_body,
        grid=(t // tm,),
        in_specs=[row, row, pl.BlockSpec((1, d), lambda i: (0, 0))],
        out_specs=out_specs,
        out_shape=out_shape,
        compiler_params=_params("parallel"),
        name="add_rmsnorm",
    )(a, b, g.reshape(1, d).astype(F32))


def _mm_body(a_ref, w_ref, o_ref):
    o_ref[...] = jnp.dot(a_ref[...], w_ref[...], preferred_element_type=F32).astype(o_ref.dtype)


def _mm_res_body(a_ref, w_ref, r_ref, o_ref):
    acc = jnp.dot(a_ref[...], w_ref[...], preferred_element_type=F32)
    o_ref[...] = (r_ref[...] + acc).astype(o_ref.dtype)


def mm_call(a, w, out_dtype, res=None, tm=1024, tn=512, name="mm"):
    m, k = a.shape
    n = w.shape[1]
    tm, tn = min(tm, m), min(tn, n)
    in_specs = [pl.BlockSpec((tm, k), lambda i, j: (i, 0)), pl.BlockSpec((k, tn), lambda i, j: (0, j))]
    args = [a, w]
    body = _mm_body
    if res is not None:
        in_specs.append(pl.BlockSpec((tm, tn), lambda i, j: (i, j)))
        args.append(res)
        body = _mm_res_body
    return pl.pallas_call(
        body,
        grid=(m // tm, n // tn),
        in_specs=in_specs,
        out_specs=pl.BlockSpec((tm, tn), lambda i, j: (i, j)),
        out_shape=jax.ShapeDtypeStruct((m, n), out_dtype),
        compiler_params=_params("parallel", "arbitrary"),
        name=name,
    )(*args)


def _hgrn_consts():
    c = HGRN_TILE
    r = np.arange(c)
    t, i = r[:, None], r[None, :]
    mats_q, mats_k, masks = [], [], []
    w = c // 2
    while w >= 1:
        seg = (r // (2 * w)) * (2 * w)
        second = (r % (2 * w)) >= w
        mats_q.append(second[:, None] & (i >= (seg + w)[:, None]) & (i <= t))
        mats_k.append(~second[:, None] & (i >= t + 1) & (i <= (seg + w - 1)[:, None]))
        masks.append((seg[:, None] == seg[None, :]) & second[:, None] & ~second[None, :])
        w //= 2
    masks.append(t == i)
    mats = [i <= t, i > t] + mats_q + mats_k
    return (jnp.asarray(np.stack(mats).astype(np.float32), BF16),
            jnp.asarray(np.stack(masks).astype(np.float32), F32), len(mats_q))


def _hgrn_body(q_ref, f_ref, i_ref, g_ref, am_ref, mask_ref, prm_ref, o_ref,
               st_ref, qt_ref, kt_ref, *, levels):
    hd = HGRN_HEAD_DIM

    @pl.when(pl.program_id(1) == 0)
    def _():
        st_ref[...] = jnp.zeros_like(st_ref)

    z = f_ref[...]
    log_lb, log_1m_lb, one_m_lb, norm_g = prm_ref[0:1], prm_ref[1:2], prm_ref[2:3], prm_ref[3:4]
    e = jnp.exp(-jnp.abs(z))
    log_sig = jnp.minimum(z, 0.0) - jnp.log1p(e)
    cc = log_1m_lb + log_sig
    log_f = jnp.maximum(log_lb, cc) + jnp.log1p(jnp.exp(-jnp.abs(log_lb - cc)))
    k = one_m_lb * jnp.where(z >= 0.0, e, 1.0) / (1.0 + e)
    q = q_ref[...]
    lf_hi = log_f.astype(BF16)
    lf_lo = (log_f - lf_hi.astype(F32)).astype(BF16)

    def exponent(m):
        a = am_ref[m]
        return jnp.dot(a, lf_hi, preferred_element_type=F32) + jnp.dot(a, lf_lo, preferred_element_type=F32)

    eb = jnp.exp(exponent(0))
    decay_all = eb[HGRN_TILE - 1:HGRN_TILE, :]
    qt_ref[0] = (q * eb).astype(BF16)
    kt_ref[0] = (k * jnp.exp(exponent(1))).astype(BF16)
    for lv in range(levels):
        qt_ref[1 + lv] = (q * jnp.exp(exponent(2 + lv))).astype(BF16)
        kt_ref[1 + lv] = (k * jnp.exp(exponent(2 + levels + lv))).astype(BF16)
    qt_ref[1 + levels] = q.astype(BF16)
    kt_ref[1 + levels] = k.astype(BF16)

    for h in range(HGRN_HEADS):
        hs = slice(h * hd, (h + 1) * hd)
        scores = None
        for lv in range(levels + 1):
            sc = lax.dot_general(qt_ref[1 + lv, :, hs], kt_ref[1 + lv, :, hs], NT_DIMS,
                                 preferred_element_type=F32) * mask_ref[lv]
            scores = sc if scores is None else scores + sc
        v = i_ref[:, hs].astype(BF16)
        st = st_ref[h]
        o = jnp.dot(scores.astype(BF16), v, preferred_element_type=F32)
        o = o + lax.dot_general(qt_ref[0, :, hs], st.astype(BF16), NT_DIMS, preferred_element_type=F32)
        st_ref[h] = st * decay_all[:, hs] + lax.dot_general(v, kt_ref[0, :, hs], TN_DIMS,
                                                            preferred_element_type=F32)
        o = o * lax.rsqrt(jnp.mean(o * o, axis=-1, keepdims=True) + EPS) * norm_g[:, hs]
        o_ref[:, hs] = (o * jax.nn.silu(g_ref[:, hs])).astype(o_ref.dtype)


def hgrn_call(proj, lb, norm_g, bsz):
    t = proj.shape[0]
    nblk = t // bsz // HGRN_TILE
    am, masks, levels = _hgrn_consts()
    lb = lb.astype(F32)
    prm = jnp.stack([jnp.log(lb), jnp.log1p(-lb), 1.0 - lb, norm_g.astype(F32)])
    col = lambda c: pl.BlockSpec((HGRN_TILE, D_HGRN), lambda b, n: (b * nblk + n, c))
    whole = lambda a: pl.BlockSpec(a.shape, lambda b, n: (0,) * a.ndim)
    return pl.pallas_call(
        functools.partial(_hgrn_body, levels=levels),
        grid=(bsz, nblk),
        in_specs=[col(0), col(1), col(2), col(3), whole(am), whole(masks), whole(prm)],
        out_specs=pl.BlockSpec((HGRN_TILE, D_HGRN), lambda b, n: (b * nblk + n, 0)),
        out_shape=jax.ShapeDtypeStruct((t, D_HGRN), BF16),
        scratch_shapes=[pltpu.VMEM((HGRN_HEADS, HGRN_HEAD_DIM, HGRN_HEAD_DIM), F32),
                        pltpu.VMEM((levels + 2, HGRN_TILE, D_HGRN), BF16),
                        pltpu.VMEM((levels + 2, HGRN_TILE, D_HGRN), BF16)],
        compiler_params=_params("parallel", "arbitrary"),
        name="hgrn2",
    )(proj, proj, proj, proj, am, masks, prm)


def _s5_prep(a_re, a_im, log_dt, b_re, b_im, c_re, c_im):
    ln, hp = SSM_CHUNK, lax.Precision.HIGHEST
    a = lax.complex(a_re.astype(F32), a_im.astype(F32))
    adt = a * jnp.exp(log_dt.astype(F32))[:, None]
    a_bar = jnp.exp(adt)
    b_bar = ((a_bar - 1.0) / a)[..., None] * lax.complex(b_re.astype(F32), b_im.astype(F32))
    c_mat = lax.complex(c_re.astype(F32), c_im.astype(F32))
    steps = jnp.arange(ln + 1, dtype=F32)
    pw = jnp.exp(adt[:, None, :] * steps[None, :, None])
    g, p, m = b_bar.shape
    kern = jnp.einsum('gmp,gtp,gpn->gtmn', c_mat, pw[:, :ln], b_bar, precision=hp).real
    lag = np.arange(ln)[:, None] - np.arange(ln)[None, :]
    toe = jnp.where((lag >= 0)[None, :, :, None, None], kern[:, np.clip(lag, 0, None)], 0.0)
    tg_t = toe.transpose(0, 2, 4, 1, 3).reshape(g, ln * m, ln * m)
    inc = pw[:, ln - 1::-1][:, :ln, None, :] * b_bar.transpose(0, 2, 1)[:, None, :, :]
    inc = inc.reshape(g, ln * m, p)
    m_b = jnp.concatenate([inc.real, inc.imag], axis=-1)
    out = c_mat.transpose(0, 2, 1)[:, :, None, :] * pw[:, 1:ln + 1].transpose(0, 2, 1)[:, :, :, None]
    out = out.reshape(g, p, ln * m)
    m_c = jnp.concatenate([out.real, -out.imag], axis=1)
    cp = jnp.exp(adt[:, None, :] * (ln * jnp.arange(9, dtype=F32))[None, :, None])
    rr = jnp.concatenate([cp.real, cp.real], axis=-1)
    ii = jnp.concatenate([-cp.imag, cp.imag], axis=-1)
    sel = np.array([1, 2, 4, 8])
    return (tg_t.astype(BF16), m_b.astype(BF16), m_c.astype(BF16),
            rr[:, :8], ii[:, :8], rr[:, sel], ii[:, sel])


def _s5_body(u_ref, tg_ref, mb_ref, mc_ref, prr_ref, pii_ref, srr_ref, sii_ref, y_ref, d_ref, x_ref, *, chains):
    u = u_ref[0]
    d_ref[...] = jnp.dot(u, mb_ref[0], preferred_element_type=F32)
    half = SSM_STATE

    def cmul(rr, ii, zz):
        return rr * zz + ii * pltpu.roll(zz, half, axis=1)

    row = lax.broadcasted_iota(jnp.int32, (8, 2 * half), 0)
    prr, pii, srr, sii = prr_ref[0], pii_ref[0], srr_ref[0], sii_ref[0]
    tiles = d_ref.shape[0] // 8 // chains
    for b in range(chains):
        carry = jnp.zeros((1, 2 * half), F32)
        for tl in range(tiles):
            r0 = (b * tiles + tl) * 8
            pre = d_ref[r0:r0 + 8, :]
            for n, s in enumerate((1, 2, 4)):
                shifted = jnp.where(row >= s, pltpu.roll(pre, s, axis=0), 0.0)
                pre = pre + cmul(srr[n:n + 1], sii[n:n + 1], shifted)
            start = cmul(prr, pii, jnp.broadcast_to(carry, pre.shape))
            x_ref[r0:r0 + 8, :] = start + jnp.where(row >= 1, pltpu.roll(pre, 1, axis=0), 0.0)
            carry = cmul(srr[3:4], sii[3:4], carry) + pre[7:8, :]
    y = jnp.dot(u, tg_ref[0], preferred_element_type=F32)
    y_ref[0] = y + jnp.dot(x_ref[...].astype(BF16), mc_ref[0], preferred_element_type=F32)


def _s5_post_body(y_ref, u_ref, prm_ref, w_ref, o_ref):
    y = y_ref[...] + prm_ref[0:1] * u_ref[...]
    y = jax.nn.gelu(y, approximate=True)
    gate = jnp.dot(y.astype(BF16), w_ref[...], preferred_element_type=F32)
    o_ref[...] = _rms(y * jax.nn.sigmoid(gate), prm_ref[1:2]).astype(o_ref.dtype)


def s5_call(proj, ucol, tables, d_skip, w_glu, norm_g, bsz, tm=512):
    t = proj.shape[0]
    rows, width = t // SSM_CHUNK, SSM_CHUNK * SSM_GROUP
    u = proj[:, ucol * D_SSM:(ucol + 1) * D_SSM]
    ug = u.reshape(rows, SSM_CHUNK, SSM_GROUPS, SSM_GROUP).transpose(2, 0, 1, 3).reshape(SSM_GROUPS, rows, width)
    grp = lambda a: pl.BlockSpec((1,) + a.shape[1:], lambda g: (g,) + (0,) * (a.ndim - 1))
    args = (ug.astype(BF16),) + tuple(tables)
    y = pl.pallas_call(
        functools.partial(_s5_body, chains=bsz),
        grid=(SSM_GROUPS,),
        in_specs=[grp(a) for a in args],
        out_specs=pl.BlockSpec((1, rows, width), lambda g: (g, 0, 0)),
        out_shape=jax.ShapeDtypeStruct((SSM_GROUPS, rows, width), F32),
        scratch_shapes=[pltpu.VMEM((rows, 2 * SSM_STATE), F32), pltpu.VMEM((rows, 2 * SSM_STATE), F32)],
        compiler_params=_params("parallel"),
        name="s5_scan",
    )(*args)
    y = y.reshape(SSM_GROUPS, rows, SSM_CHUNK, SSM_GROUP).transpose(1, 2, 0, 3).reshape(t, D_SSM)
    tm = min(tm, t)
    prm = jnp.stack([d_skip.astype(F32), norm_g.astype(F32)])
    return pl.pallas_call(
        _s5_post_body,
        grid=(t // tm,),
        in_specs=[pl.BlockSpec((tm, D_SSM), lambda i: (i, 0)),
                  pl.BlockSpec((tm, D_SSM), lambda i: (i, ucol)),
                  pl.BlockSpec((2, D_SSM), lambda i: (0, 0)),
                  pl.BlockSpec((D_SSM, D_SSM), lambda i: (0, 0))],
        out_specs=pl.BlockSpec((tm, D_SSM), lambda i: (i, 0)),
        out_shape=jax.ShapeDtypeStruct((t, D_SSM), BF16),
        compiler_params=_params("parallel"),
        name="s5_post",
    )(y, proj, prm, w_glu.astype(BF16))


def _att_bias(rel_bias):
    qi = np.arange(ATT_TQ)[:, None]
    kj = np.arange(ATT_WIN)[None, :]
    rel = np.clip(qi + LEFT_CHUNKS * CHUNK - kj, -MAX_REL, MAX_REL) + MAX_REL
    back = kj // CHUNK - qi // CHUNK
    valid = (back >= 0) & (back <= LEFT_CHUNKS)
    return jnp.where(valid[None], rel_bias.astype(F32)[:, rel], -1e30)


def _att_body(q_ref, k_ref, v_ref, b_ref, o_ref):
    i = pl.program_id(2)
    scale = ATT_HEAD_DIM ** -0.5
    q = q_ref[0]

    def attend(kw, vw, bias):
        s = lax.dot_general(q, kw, NT_DIMS, preferred_element_type=F32) * scale + bias
        p = jnp.exp(s - jnp.max(s, axis=-1, keepdims=True))
        l = jnp.sum(p, axis=-1, keepdims=True)
        o_ref[0] = jnp.dot(p.astype(BF16), vw, preferred_element_type=F32) / l

    lead = ATT_WIN // ATT_TQ - 1
    for n in range(lead):
        @pl.when(i == n)
        def _(n=n):
            nk = (n + 1) * ATT_TQ
            attend(k_ref[0, 0:nk, :], v_ref[0, 0:nk, :], b_ref[0, :, ATT_WIN - nk:])

    @pl.when(i >= lead)
    def _():
        start = pl.multiple_of(i * ATT_TQ - (ATT_WIN - ATT_TQ), ATT_TQ)
        attend(k_ref[0, pl.ds(start, ATT_WIN), :], v_ref[0, pl.ds(start, ATT_WIN), :], b_ref[0])


def attention_call(qkv, rel_bias, norm_g, bsz):
    t = qkv.shape[0]
    seq = t // bsz
    x = qkv.reshape(bsz, seq, 3 * D_ATT)
    bias = _att_bias(rel_bias)
    o = pl.pallas_call(
        _att_body,
        grid=(bsz, ATT_HEADS, seq // ATT_TQ),
        in_specs=[pl.BlockSpec((1, ATT_TQ, ATT_HEAD_DIM), lambda b, h, i: (b, i, h)),
                  pl.BlockSpec((1, seq, ATT_HEAD_DIM), lambda b, h, i: (b, 0, ATT_HEADS + h)),
                  pl.BlockSpec((1, seq, ATT_HEAD_DIM), lambda b, h, i: (b, 0, 2 * ATT_HEADS + h)),
                  pl.BlockSpec((1, ATT_TQ, ATT_WIN), lambda b, h, i: (h, 0, 0))],
        out_specs=pl.BlockSpec((1, ATT_TQ, ATT_HEAD_DIM), lambda b, h, i: (b, i, h)),
        out_shape=jax.ShapeDtypeStruct((bsz, seq, D_ATT), F32),
        compiler_params=_params("parallel", "parallel", "arbitrary"),
        name="chunk_attention",
    )(x, x, x, bias)
    return rmsnorm_call(o.reshape(t, D_ATT), norm_g, BF16)


def _top_rows(x, k):
    rows = []
    idx = lax.broadcasted_iota(jnp.int32, x.shape, 0).astype(F32)
    for i in range(k):
        m = jnp.max(x, axis=0, keepdims=True)
        rows.append(m)
        if i + 1 < k:
            first = jnp.min(jnp.where(x == m, idx, float(x.shape[0])), axis=0, keepdims=True)
            x = jnp.where(idx == first, -jnp.inf, x)
    return rows


def _peer_topk_body(q_ref, k1_ref, k2_ref, a1_ref, a2_ref, r_ref):
    half = PEER_QDIM // 2
    for h in range(PEER_HEADS):
        base = h * PEER_QDIM
        s1 = lax.dot_general(k1_ref[...], q_ref[:, base:base + half], NT_DIMS, preferred_element_type=F32)
        s2 = lax.dot_general(k2_ref[...], q_ref[:, base + half:base + PEER_QDIM], NT_DIMS,
                             preferred_element_type=F32)
        a1_ref[h] = s1
        a2_ref[h] = s2
        v1 = _top_rows(s1, PEER_TOPK)
        v2 = jnp.concatenate(_top_rows(s2, PEER_TOPK), axis=0)
        cand = jnp.concatenate([v1[a] + v2[:PEER_TOPK // (a + 1)] for a in range(PEER_TOPK)], axis=0)
        top = _top_rows(cand, PEER_TOPK)
        z = jnp.ones_like(top[0])
        for t in top[1:]:
            z = z + jnp.exp(t - top[0])
        r_ref[0, h:h + 1, :] = top[PEER_TOPK - 1]
        r_ref[1, h:h + 1, :] = v1[0]
        r_ref[2, h:h + 1, :] = v2[0:1]
        r_ref[3, h:h + 1, :] = 1.0 / z


def peer_topk_call(q, keys1, keys2, tm=256):
    t = q.shape[0]
    hk = (PEER_HEADS, PEER_KEYS, t)
    return pl.pallas_call(
        _peer_topk_body,
        grid=(t // tm,),
        in_specs=[pl.BlockSpec((tm, PEER_HEADS * PEER_QDIM), lambda i: (i, 0)),
                  pl.BlockSpec((PEER_KEYS, PEER_QDIM // 2), lambda i: (0, 0)),
                  pl.BlockSpec((PEER_KEYS, PEER_QDIM // 2), lambda i: (0, 0))],
        out_specs=[pl.BlockSpec((PEER_HEADS, PEER_KEYS, tm), lambda i: (0, 0, i)),
                   pl.BlockSpec((PEER_HEADS, PEER_KEYS, tm), lambda i: (0, 0, i)),
                   pl.BlockSpec((4, PEER_HEADS, tm), lambda i: (0, 0, i))],
        out_shape=[jax.ShapeDtypeStruct(hk, F32), jax.ShapeDtypeStruct(hk, F32),
                   jax.ShapeDtypeStruct((4, PEER_HEADS, t), F32)],
        compiler_params=_params("parallel"),
        name="peer_topk",
    )(q, keys1, keys2)


def _peer_mix_body(x_ref, u_ref, v_ref, a1_ref, a2_ref, r_ref, o_ref, e2_ref, c1_ref, *, te):
    j = pl.program_id(1)

    @pl.when(j == 0)
    def _():
        for h in range(PEER_HEADS):
            e2_ref[h] = jnp.exp(a2_ref[h] - r_ref[2, h:h + 1, :])
            c1_ref[h] = jnp.exp(a1_ref[h] - r_ref[1, h:h + 1, :]) * r_ref[3, h:h + 1, :]
        o_ref[...] = jnp.zeros_like(o_ref)

    ht = lax.dot_general(u_ref[...], x_ref[...], NT_DIMS, preferred_element_type=F32)
    parts = []
    for s in range(te // PEER_KEYS):
        i1 = j * (te // PEER_KEYS) + s
        gate = None
        for h in range(PEER_HEADS):
            sums = a1_ref[h, pl.ds(i1, 1), :] + a2_ref[h]
            term = jnp.where(sums >= r_ref[0, h:h + 1, :], e2_ref[h], 0.0) * c1_ref[h, pl.ds(i1, 1), :]
            gate = term if gate is None else gate + term
        act = jax.nn.gelu(ht[s * PEER_KEYS:(s + 1) * PEER_KEYS], approximate=True)
        parts.append((gate * act).astype(BF16))
    w = jnp.concatenate(parts, axis=0)
    o_ref[...] += lax.dot_general(w, v_ref[...], TN_DIMS, preferred_element_type=F32)


def peer_mix_call(hn, u, v, a1, a2, r, tm=512, te=256):
    t, d = hn.shape
    e = u.shape[0]
    tm = min(tm, t)
    hk = (PEER_HEADS, PEER_KEYS, tm)
    return pl.pallas_call(
        functools.partial(_peer_mix_body, te=te),
        grid=(t // tm, e // te),
        in_specs=[pl.BlockSpec((tm, d), lambda i, j: (i, 0)),
                  pl.BlockSpec((te, d), lambda i, j: (j, 0)),
                  pl.BlockSpec((te, d), lambda i, j: (j, 0)),
                  pl.BlockSpec(hk, lambda i, j: (0, 0, i)),
                  pl.BlockSpec(hk, lambda i, j: (0, 0, i)),
                  pl.BlockSpec((4, PEER_HEADS, tm), lambda i, j: (0, 0, i))],
        out_specs=pl.BlockSpec((tm, d), lambda i, j: (i, 0)),
        out_shape=jax.ShapeDtypeStruct((t, d), F32),
        scratch_shapes=[pltpu.VMEM(hk, F32), pltpu.VMEM(hk, F32)],
        compiler_params=_params("parallel", "arbitrary"),
        name="peer_mix",
    )(hn, u, v, a1, a2, r)


def peer_ffn_call(h, norm_g, w_q, keys1, keys2, u, v):
    hn = rmsnorm_call(h, norm_g, BF16)
    q = mm_call(hn, w_q.astype(BF16), BF16, name="peer_q")
    a1, a2, r = peer_topk_call(q, keys1.astype(BF16), keys2.astype(BF16))
    return peer_mix_call(hn, u.astype(BF16), v.astype(BF16), a1, a2, r)


def mixer_call(xn, h, w_in, w_out, lb, hgrn_norm, s5_tables, ssm_d, ssm_w_glu, ssm_norm, rel_bias, att_norm, bsz):
    n_a = 4 * D_HGRN + D_SSM
    proj_a = mm_call(xn, w_in[:, :n_a].astype(BF16), F32, name="in_proj_a")
    proj_c = mm_call(xn, w_in[:, n_a:].astype(BF16), BF16, name="in_proj_c")
    y_a = hgrn_call(proj_a, lb, hgrn_norm, bsz)
    y_b = s5_call(proj_a, 4, s5_tables, ssm_d, ssm_w_glu, ssm_norm, bsz)
    y_c = attention_call(proj_c, rel_bias, att_norm, bsz)
    mixed = jnp.concatenate([y_a, y_b, y_c], axis=-1)
    return mm_call(mixed, w_out.astype(BF16), F32, res=h, name="out_proj")


def kernel(x, w_in, w_out, hgrn_lb, hgrn_norm, ssm_a_re, ssm_a_im, ssm_log_dt, ssm_b_re, ssm_b_im, ssm_c_re, ssm_c_im, ssm_d, ssm_w_glu, ssm_norm, att_rel_bias, att_norm, norm_mix, norm_ffn, peer_w_q, peer_keys1, peer_keys2, peer_u, peer_v, norm_final):
    bsz, seq, d = x.shape
    lb_all = jnp.cumsum(jax.nn.softmax(hgrn_lb.astype(F32), axis=0), axis=0)
    lb_all = lb_all - lb_all[0]
    h = x.reshape(bsz * seq, d)
    xn = rmsnorm_call(h, norm_mix[0], BF16)
    out = None
    for layer in range(DEPTH):
        tables = _s5_prep(ssm_a_re[layer], ssm_a_im[layer], ssm_log_dt[layer], ssm_b_re[layer],
                          ssm_b_im[layer], ssm_c_re[layer], ssm_c_im[layer])
        h = mixer_call(xn, h, w_in[layer], w_out[layer], lb_all[layer], hgrn_norm[layer], tables,
                       ssm_d[layer], ssm_w_glu[layer], ssm_norm[layer], att_rel_bias[layer],
                       att_norm[layer], bsz)
        ffn = peer_ffn_call(h, norm_ffn[layer], peer_w_q[layer], peer_keys1[layer], peer_keys2[layer],
                            peer_u[layer], peer_v[layer])
        if layer + 1 < DEPTH:
            h, xn = add_rmsnorm_call(h, ffn, norm_mix[layer + 1], BF16, keep_sum=True)
        else:
            out = add_rmsnorm_call(h, ffn, norm_final, F32, keep_sum=False)
    return out.reshape(bsz, seq, d)
```

```python
import functools
import math

import jax
import jax.numpy as jnp
import numpy as np
from jax import lax
from jax.experimental import pallas as pl
from jax.experimental.pallas import tpu as pltpu

F32 = jnp.float32
BF16 = jnp.bfloat16

EPS = 1e-6
DEPTH = 2
CHUNK = 64
D_HGRN = 1024
HGRN_HEADS = 8
HGRN_HEAD_DIM = D_HGRN // HGRN_HEADS
HGRN_TILE = 128
D_SSM = 1024
SSM_GROUP = 16
SSM_GROUPS = D_SSM // SSM_GROUP
SSM_STATE = 64
SSM_CHUNK = 16
D_ATT = 2048
ATT_HEADS = 16
ATT_HEAD_DIM = D_ATT // ATT_HEADS
LEFT_CHUNKS = 8
MAX_REL = 256
ATT_TQ = 256
ATT_WIN = ATT_TQ + LEFT_CHUNKS * CHUNK
PEER_HEADS = 8
PEER_KEYS = 128
PEER_QDIM = 256
PEER_TOPK = 16

VMEM_LIMIT = 56 * 1024 * 1024

NT_DIMS = (((1,), (1,)), ((), ()))
TN_DIMS = (((0,), (0,)), ((), ()))


def _params(*sem):
    return pltpu.CompilerParams(dimension_semantics=sem, vmem_limit_bytes=VMEM_LIMIT)


def _rms(x, g):
    ms = jnp.mean(x * x, axis=-1, keepdims=True)
    return x * lax.rsqrt(ms + EPS) * g


def _rmsnorm_body(x_ref, g_ref, o_ref):
    o_ref[...] = _rms(x_ref[...], g_ref[...]).astype(o_ref.dtype)


def _add_rmsnorm_body(a_ref, b_ref, g_ref, *out_refs):
    s = a_ref[...] + b_ref[...]
    if len(out_refs) == 2:
        out_refs[0][...] = s
    out_refs[-1][...] = _rms(s, g_ref[...]).astype(out_refs[-1].dtype)


def rmsnorm_call(x, g, out_dtype, tm=256):
    t, d = x.shape
    row = pl.BlockSpec((tm, d), lambda i: (i, 0))
    return pl.pallas_call(
        _rmsnorm_body,
        grid=(t // tm,),
        in_specs=[row, pl.BlockSpec((1, d), lambda i: (0, 0))],
        out_specs=row,
        out_shape=jax.ShapeDtypeStruct((t, d), out_dtype),
        compiler_params=_params("parallel"),
        name="rmsnorm",
    )(x, g.reshape(1, d).astype(F32))


def add_rmsnorm_call(a, b, g, out_dtype, keep_sum, tm=256):
    t, d = a.shape
    row = pl.BlockSpec((tm, d), lambda i: (i, 0))
    out_specs = [row, row] if keep_sum else row
    normed = jax.ShapeDtypeStruct((t, d), out_dtype)
    out_shape = [jax.ShapeDtypeStruct((t, d), F32), normed] if keep_sum else normed
    return pl.pallas_call(
        _add_rmsnorm_body,
        grid=(t // tm,),
        in_specs=[row, row, pl.BlockSpec((1, d), lambda i: (0, 0))],
        out_specs=out_specs,
        out_shape=out_shape,
        compiler_params=_params("parallel"),
        name="add_rmsnorm",
    )(a, b, g.reshape(1, d).astype(F32))


def _mm_body(a_ref, w_ref, o_ref):
    o_ref[...] = jnp.dot(a_ref[...], w_ref[...], preferred_element_type=F32).astype(o_ref.dtype)


def _mm_res_body(a_ref, w_ref, r_ref, o_ref):
    acc = jnp.dot(a_ref[...], w_ref[...], preferred_element_type=F32)
    o_ref[...] = (r_ref[...] + acc).astype(o_ref.dtype)


def mm_call(a, w, out_dtype, res=None, tm=1024, tn=512, name="mm"):
    m, k = a.shape
    n = w.shape[1]
    tm, tn = min(tm, m), min(tn, n)
    in_specs = [pl.BlockSpec((tm, k), lambda i, j: (i, 0)), pl.BlockSpec((k, tn), lambda i, j: (0, j))]
    args = [a, w]
    body = _mm_body
    if res is not None:
        in_specs.append(pl.BlockSpec((tm, tn), lambda i, j: (i, j)))
        args.append(res)
        body = _mm_res_body
    return pl.pallas_call(
        body,
        grid=(m // tm, n // tn),
        in_specs=in_specs,
        out_specs=pl.BlockSpec((tm, tn), lambda i, j: (i, j)),
        out_shape=jax.ShapeDtypeStruct((m, n), out_dtype),
        compiler_params=_params("parallel", "arbitrary"),
        name=name,
    )(*args)


def _hgrn_consts():
    c = HGRN_TILE
    r = np.arange(c)
    t, i = r[:, None], r[None, :]
    mats_q, mats_k, masks = [], [], []
    w = c // 2
    while w >= 1:
        seg = (r // (2 * w)) * (2 * w)
        second = (r % (2 * w)) >= w
        mats_q.append(second[:, None] & (i >= (seg + w)[:, None]) & (i <= t))
        mats_k.append(~second[:, None] & (i >= t + 1) & (i <= (seg + w - 1)[:, None]))
        masks.append((seg[:, None] == seg[None, :]) & second[:, None] & ~second[None, :])
        w //= 2
    masks.append(t == i)
    mats = [i <= t, i > t] + mats_q + mats_k
    return (jnp.asarray(np.stack(mats).astype(np.float32), BF16),
            jnp.asarray(np.stack(masks).astype(np.float32), F32), len(mats_q))


def _hgrn_body(q_ref, f_ref, i_ref, g_ref, am_ref, mask_ref, prm_ref, o_ref,
               st_ref, qt_ref, kt_ref, *, levels):
    hd = HGRN_HEAD_DIM

    @pl.when(pl.program_id(1) == 0)
    def _():
        st_ref[...] = jnp.zeros_like(st_ref)

    z = f_ref[...]
    log_lb, log_1m_lb, one_m_lb, norm_g = prm_ref[0:1], prm_ref[1:2], prm_ref[2:3], prm_ref[3:4]
    e = jnp.exp(-jnp.abs(z))
    log_sig = jnp.minimum(z, 0.0) - jnp.log1p(e)
    cc = log_1m_lb + log_sig
    log_f = jnp.maximum(log_lb, cc) + jnp.log1p(jnp.exp(-jnp.abs(log_lb - cc)))
    k = one_m_lb * jnp.where(z >= 0.0, e, 1.0) / (1.0 + e)
    q = q_ref[...]
    lf_hi = log_f.astype(BF16)
    lf_lo = (log_f - lf_hi.astype(F32)).astype(BF16)

    def exponent(m):
        a = am_ref[m]
        return jnp.dot(a, lf_hi, preferred_element_type=F32) + jnp.dot(a, lf_lo, preferred_element_type=F32)

    eb = jnp.exp(exponent(0))
    decay_all = eb[HGRN_TILE - 1:HGRN_TILE, :]
    qt_ref[0] = (q * eb).astype(BF16)
    kt_ref[0] = (k * jnp.exp(exponent(1))).astype(BF16)
    for lv in range(levels):
        qt_ref[1 + lv] = (q * jnp.exp(exponent(2 + lv))).astype(BF16)
        kt_ref[1 + lv] = (k * jnp.exp(exponent(2 + levels + lv))).astype(BF16)
    qt_ref[1 + levels] = q.astype(BF16)
    kt_ref[1 + levels] = k.astype(BF16)

    for h in range(HGRN_HEADS):
        hs = slice(h * hd, (h + 1) * hd)
        scores = None
        for lv in range(levels + 1):
            sc = lax.dot_general(qt_ref[1 + lv, :, hs], kt_ref[1 + lv, :, hs], NT_DIMS,
                                 preferred_element_type=F32) * mask_ref[lv]
            scores = sc if scores is None else scores + sc
        v = i_ref[:, hs].astype(BF16)
        st = st_ref[h]
        o = jnp.dot(scores.astype(BF16), v, preferred_element_type=F32)
        o = o + lax.dot_general(qt_ref[0, :, hs], st.astype(BF16), NT_DIMS, preferred_element_type=F32)
        st_ref[h] = st * decay_all[:, hs] + lax.dot_general(v, kt_ref[0, :, hs], TN_DIMS,
                                                            preferred_element_type=F32)
        o = o * lax.rsqrt(jnp.mean(o * o, axis=-1, keepdims=True) + EPS) * norm_g[:, hs]
        o_ref[:, hs] = (o * jax.nn.silu(g_ref[:, hs])).astype(o_ref.dtype)


def hgrn_call(proj, lb, norm_g, bsz):
    t = proj.shape[0]
    nblk = t // bsz // HGRN_TILE
    am, masks, levels = _hgrn_consts()
    lb = lb.astype(F32)
    prm = jnp.stack([jnp.log(lb), jnp.log1p(-lb), 1.0 - lb, norm_g.astype(F32)])
    col = lambda c: pl.BlockSpec((HGRN_TILE, D_HGRN), lambda b, n: (b * nblk + n, c))
    whole = lambda a: pl.BlockSpec(a.shape, lambda b, n: (0,) * a.ndim)
    return pl.pallas_call(
        functools.partial(_hgrn_body, levels=levels),
        grid=(bsz, nblk),
        in_specs=[col(0), col(1), col(2), col(3), whole(am), whole(masks), whole(prm)],
        out_specs=pl.BlockSpec((HGRN_TILE, D_HGRN), lambda b, n: (b * nblk + n, 0)),
        out_shape=jax.ShapeDtypeStruct((t, D_HGRN), BF16),
        scratch_shapes=[pltpu.VMEM((HGRN_HEADS, HGRN_HEAD_DIM, HGRN_HEAD_DIM), F32),
                        pltpu.VMEM((levels + 2, HGRN_TILE, D_HGRN), BF16),
                        pltpu.VMEM((levels + 2, HGRN_TILE, D_HGRN), BF16)],
        compiler_params=_params("parallel", "arbitrary"),
        name="hgrn2",
    )(proj, proj, proj, proj, am, masks, prm)


def _s5_prep(a_re, a_im, log_dt, b_re, b_im, c_re, c_im):
    ln, hp = SSM_CHUNK, lax.Precision.HIGHEST
    a = lax.complex(a_re.astype(F32), a_im.astype(F32))
    adt = a * jnp.exp(log_dt.astype(F32))[:, None]
    a_bar = jnp.exp(adt)
    b_bar = ((a_bar - 1.0) / a)[..., None] * lax.complex(b_re.astype(F32), b_im.astype(F32))
    c_mat = lax.complex(c_re.astype(F32), c_im.astype(F32))
    steps = jnp.arange(ln + 1, dtype=F32)
    pw = jnp.exp(adt[:, None, :] * steps[None, :, None])
    g, p, m = b_bar.shape
    kern = jnp.einsum('gmp,gtp,gpn->gtmn', c_mat, pw[:, :ln], b_bar, precision=hp).real
    lag = np.arange(ln)[:, None] - np.arange(ln)[None, :]
    place = jnp.asarray((lag[:, :, None] == np.arange(ln)[None, None, :]).astype(np.float32))
    toe = jnp.einsum('abt,gtmn->gabmn', place, kern, precision=hp)
    tg_t = toe.transpose(0, 2, 4, 1, 3).reshape(g, ln * m, ln * m)
    inc = pw[:, ln - 1::-1][:, :ln, None, :] * b_bar.transpose(0, 2, 1)[:, None, :, :]
    inc = inc.reshape(g, ln * m, p)
    m_b = jnp.concatenate([inc.real, inc.imag], axis=-1)
    out = c_mat.transpose(0, 2, 1)[:, :, None, :] * pw[:, 1:ln + 1].transpose(0, 2, 1)[:, :, :, None]
    out = out.reshape(g, p, ln * m)
    m_c = jnp.concatenate([out.real, -out.imag], axis=1)
    cp = jnp.exp(adt[:, None, :] * (ln * jnp.arange(9, dtype=F32))[None, :, None])
    rr = jnp.concatenate([cp.real, cp.real], axis=-1)
    ii = jnp.concatenate([-cp.imag, cp.imag], axis=-1)
    sel = np.array([1, 2, 4, 8])
    return (tg_t.astype(BF16), m_b.astype(BF16), m_c.astype(BF16),
            rr[:, :8], ii[:, :8], rr[:, sel], ii[:, sel])


def _s5_body(u_ref, tg_ref, mb_ref, mc_ref, prr_ref, pii_ref, srr_ref, sii_ref, y_ref, d_ref, x_ref, *, chains):
    u = u_ref[0]
    d_ref[...] = jnp.dot(u, mb_ref[0], preferred_element_type=F32)
    half = SSM_STATE

    def cmul(rr, ii, zz):
        return rr * zz + ii * pltpu.roll(zz, half, axis=1)

    row = lax.broadcasted_iota(jnp.int32, (8, 2 * half), 0)
    prr, pii, srr, sii = prr_ref[0], pii_ref[0], srr_ref[0], sii_ref[0]
    tiles = d_ref.shape[0] // 8 // chains
    for b in range(chains):
        carry = jnp.zeros((1, 2 * half), F32)
        for tl in range(tiles):
            r0 = (b * tiles + tl) * 8
            pre = d_ref[r0:r0 + 8, :]
            for n, s in enumerate((1, 2, 4)):
                shifted = jnp.where(row >= s, pltpu.roll(pre, s, axis=0), 0.0)
                pre = pre + cmul(srr[n:n + 1], sii[n:n + 1], shifted)
            start = cmul(prr, pii, jnp.broadcast_to(carry, pre.shape))
            x_ref[r0:r0 + 8, :] = start + jnp.where(row >= 1, pltpu.roll(pre, 1, axis=0), 0.0)
            carry = cmul(srr[3:4], sii[3:4], carry) + pre[7:8, :]
    y = jnp.dot(u, tg_ref[0], preferred_element_type=F32)
    y_ref[0] = y + jnp.dot(x_ref[...].astype(BF16), mc_ref[0], preferred_element_type=F32)


def _s5_post_body(y_ref, u_ref, prm_ref, w_ref, o_ref):
    y = y_ref[...] + prm_ref[0:1] * u_ref[...]
    y = jax.nn.gelu(y, approximate=True)
    gate = jnp.dot(y.astype(BF16), w_ref[...], preferred_element_type=F32)
    o_ref[...] = _rms(y * jax.nn.sigmoid(gate), prm_ref[1:2]).astype(o_ref.dtype)


def s5_call(proj, ucol, tables, d_skip, w_glu, norm_g, bsz, tm=512):
    t = proj.shape[0]
    rows, width = t // SSM_CHUNK, SSM_CHUNK * SSM_GROUP
    u = proj[:, ucol * D_SSM:(ucol + 1) * D_SSM]
    ug = u.reshape(rows, SSM_CHUNK, SSM_GROUPS, SSM_GROUP).transpose(2, 0, 1, 3).reshape(SSM_GROUPS, rows, width)
    grp = lambda a: pl.BlockSpec((1,) + a.shape[1:], lambda g: (g,) + (0,) * (a.ndim - 1))
    args = (ug.astype(BF16),) + tuple(tables)
    y = pl.pallas_call(
        functools.partial(_s5_body, chains=bsz),
        grid=(SSM_GROUPS,),
        in_specs=[grp(a) for a in args],
        out_specs=pl.BlockSpec((1, rows, width), lambda g: (g, 0, 0)),
        out_shape=jax.ShapeDtypeStruct((SSM_GROUPS, rows, width), F32),
        scratch_shapes=[pltpu.VMEM((rows, 2 * SSM_STATE), F32), pltpu.VMEM((rows, 2 * SSM_STATE), F32)],
        compiler_params=_params("parallel"),
        name="s5_scan",
    )(*args)
    y = y.reshape(SSM_GROUPS, rows, SSM_CHUNK, SSM_GROUP).transpose(1, 2, 0, 3).reshape(t, D_SSM)
    tm = min(tm, t)
    prm = jnp.stack([d_skip.astype(F32), norm_g.astype(F32)])
    return pl.pallas_call(
        _s5_post_body,
        grid=(t // tm,),
        in_specs=[pl.BlockSpec((tm, D_SSM), lambda i: (i, 0)),
                  pl.BlockSpec((tm, D_SSM), lambda i: (i, ucol)),
                  pl.BlockSpec((2, D_SSM), lambda i: (0, 0)),
                  pl.BlockSpec((D_SSM, D_SSM), lambda i: (0, 0))],
        out_specs=pl.BlockSpec((tm, D_SSM), lambda i: (i, 0)),
        out_shape=jax.ShapeDtypeStruct((t, D_SSM), BF16),
        compiler_params=_params("parallel"),
        name="s5_post",
    )(y, proj, prm, w_glu.astype(BF16))


def _att_bias(rel_bias):
    qi = np.arange(ATT_TQ)[:, None]
    kj = np.arange(ATT_WIN)[None, :]
    back = kj // CHUNK - qi // CHUNK
    valid = (back >= 0) & (back <= LEFT_CHUNKS)
    rb = rel_bias.astype(F32)
    nh = rb.shape[0]
    period = ATT_TQ + ATT_WIN - 1
    lo = MAX_REL - (ATT_TQ - 1)
    g = jnp.concatenate([rb[:, lo:], jnp.broadcast_to(rb[:, -1:], (nh, period - (2 * MAX_REL + 1 - lo)))], axis=1)
    flat = jnp.tile(g, (1, ATT_TQ + 1))[:, :ATT_TQ * (period + 1)]
    hankel = flat.reshape(nh, ATT_TQ, period + 1)[:, :, :ATT_WIN]
    return jnp.where(valid[None], hankel[:, :, ::-1], -1e30)


def _att_body(q_ref, k_ref, v_ref, b_ref, o_ref):
    i = pl.program_id(2)
    scale = ATT_HEAD_DIM ** -0.5
    q = q_ref[0]

    def attend(kw, vw, bias):
        s = lax.dot_general(q, kw, NT_DIMS, preferred_element_type=F32) * scale + bias
        p = jnp.exp(s - jnp.max(s, axis=-1, keepdims=True))
        l = jnp.sum(p, axis=-1, keepdims=True)
        o_ref[0] = jnp.dot(p.astype(BF16), vw, preferred_element_type=F32) / l

    lead = ATT_WIN // ATT_TQ - 1
    for n in range(lead):
        @pl.when(i == n)
        def _(n=n):
            nk = (n + 1) * ATT_TQ
            attend(k_ref[0, 0:nk, :], v_ref[0, 0:nk, :], b_ref[0, :, ATT_WIN - nk:])

    @pl.when(i >= lead)
    def _():
        start = pl.multiple_of(i * ATT_TQ - (ATT_WIN - ATT_TQ), ATT_TQ)
        attend(k_ref[0, pl.ds(start, ATT_WIN), :], v_ref[0, pl.ds(start, ATT_WIN), :], b_ref[0])


def attention_call(qkv, rel_bias, norm_g, bsz):
    t = qkv.shape[0]
    seq = t // bsz
    x = qkv.reshape(bsz, seq, 3 * D_ATT)
    bias = _att_bias(rel_bias)
    o = pl.pallas_call(
        _att_body,
        grid=(bsz, ATT_HEADS, seq // ATT_TQ),
        in_specs=[pl.BlockSpec((1, ATT_TQ, ATT_HEAD_DIM), lambda b, h, i: (b, i, h)),
                  pl.BlockSpec((1, seq, ATT_HEAD_DIM), lambda b, h, i: (b, 0, ATT_HEADS + h)),
                  pl.BlockSpec((1, seq, ATT_HEAD_DIM), lambda b, h, i: (b, 0, 2 * ATT_HEADS + h)),
                  pl.BlockSpec((1, ATT_TQ, ATT_WIN), lambda b, h, i: (h, 0, 0))],
        out_specs=pl.BlockSpec((1, ATT_TQ, ATT_HEAD_DIM), lambda b, h, i: (b, i, h)),
        out_shape=jax.ShapeDtypeStruct((bsz, seq, D_ATT), F32),
        compiler_params=_params("parallel", "parallel", "arbitrary"),
        name="chunk_attention",
    )(x, x, x, bias)
    return rmsnorm_call(o.reshape(t, D_ATT), norm_g, BF16)


def _top_rows(x, k):
    rows = []
    idx = lax.broadcasted_iota(jnp.int32, x.shape, 0).astype(F32)
    for i in range(k):
        m = jnp.max(x, axis=0, keepdims=True)
        rows.append(m)
        if i + 1 < k:
            first = jnp.min(jnp.where(x == m, idx, float(x.shape[0])), axis=0, keepdims=True)
            x = jnp.where(idx == first, -jnp.inf, x)
    return rows


def _peer_topk_body(q_ref, k1_ref, k2_ref, a1_ref, a2_ref, r_ref):
    half = PEER_QDIM // 2
    for h in range(PEER_HEADS):
        base = h * PEER_QDIM
        s1 = lax.dot_general(k1_ref[...], q_ref[:, base:base + half], NT_DIMS, preferred_element_type=F32)
        s2 = lax.dot_general(k2_ref[...], q_ref[:, base + half:base + PEER_QDIM], NT_DIMS,
                             preferred_element_type=F32)
        a1_ref[h] = s1
        a2_ref[h] = s2
        v1 = _top_rows(s1, PEER_TOPK)
        v2 = jnp.concatenate(_top_rows(s2, PEER_TOPK), axis=0)
        cand = jnp.concatenate([v1[a] + v2[:PEER_TOPK // (a + 1)] for a in range(PEER_TOPK)], axis=0)
        top = _top_rows(cand, PEER_TOPK)
        z = jnp.ones_like(top[0])
        for t in top[1:]:
            z = z + jnp.exp(t - top[0])
        r_ref[0, h:h + 1, :] = top[PEER_TOPK - 1]
        r_ref[1, h:h + 1, :] = v1[0]
        r_ref[2, h:h + 1, :] = v2[0:1]
        r_ref[3, h:h + 1, :] = 1.0 / z


def peer_topk_call(q, keys1, keys2, tm=256):
    t = q.shape[0]
    hk = (PEER_HEADS, PEER_KEYS, t)
    return pl.pallas_call(
        _peer_topk_body,
        grid=(t // tm,),
        in_specs=[pl.BlockSpec((tm, PEER_HEADS * PEER_QDIM), lambda i: (i, 0)),
                  pl.BlockSpec((PEER_KEYS, PEER_QDIM // 2), lambda i: (0, 0)),
                  pl.BlockSpec((PEER_KEYS, PEER_QDIM // 2), lambda i: (0, 0))],
        out_specs=[pl.BlockSpec((PEER_HEADS, PEER_KEYS, tm), lambda i: (0, 0, i)),
                   pl.BlockSpec((PEER_HEADS, PEER_KEYS, tm), lambda i: (0, 0, i)),
                   pl.BlockSpec((4, PEER_HEADS, tm), lambda i: (0, 0, i))],
        out_shape=[jax.ShapeDtypeStruct(hk, F32), jax.ShapeDtypeStruct(hk, F32),
                   jax.ShapeDtypeStruct((4, PEER_HEADS, t), F32)],
        compiler_params=_params("parallel"),
        name="peer_topk",
    )(q, keys1, keys2)


def _peer_gate_body(x_ref, u_ref, a1_ref, a2_ref, r_ref, w_ref, e2_ref, c1_ref, *, te):
    j = pl.program_id(1)

    @pl.when(j == 0)
    def _():
        for h in range(PEER_HEADS):
            e2_ref[h] = jnp.exp(a2_ref[h] - r_ref[2, h:h + 1, :])
            c1_ref[h] = jnp.exp(a1_ref[h] - r_ref[1, h:h + 1, :]) * r_ref[3, h:h + 1, :]

    ht = lax.dot_general(u_ref[...], x_ref[...], NT_DIMS, preferred_element_type=F32)
    for s in range(te // PEER_KEYS):
        i1 = j * (te // PEER_KEYS) + s
        gate = None
        for h in range(PEER_HEADS):
            sums = a1_ref[h, pl.ds(i1, 1), :] + a2_ref[h]
            term = jnp.where(sums >= r_ref[0, h:h + 1, :], e2_ref[h], 0.0) * c1_ref[h, pl.ds(i1, 1), :]
            gate = term if gate is None else gate + term
        act = jax.nn.gelu(ht[s * PEER_KEYS:(s + 1) * PEER_KEYS], approximate=True)
        w_ref[:, s * PEER_KEYS:(s + 1) * PEER_KEYS] = jnp.transpose(gate * act).astype(w_ref.dtype)


def peer_gate_call(hn, u, a1, a2, r, tm=512, te=512):
    t, d = hn.shape
    e = u.shape[0]
    tm = min(tm, t)
    hk = (PEER_HEADS, PEER_KEYS, tm)
    return pl.pallas_call(
        functools.partial(_peer_gate_body, te=te),
        grid=(t // tm, e // te),
        in_specs=[pl.BlockSpec((tm, d), lambda i, j: (i, 0)),
                  pl.BlockSpec((te, d), lambda i, j: (j, 0)),
                  pl.BlockSpec(hk, lambda i, j: (0, 0, i)),
                  pl.BlockSpec(hk, lambda i, j: (0, 0, i)),
                  pl.BlockSpec((4, PEER_HEADS, tm), lambda i, j: (0, 0, i))],
        out_specs=pl.BlockSpec((tm, te), lambda i, j: (i, j)),
        out_shape=jax.ShapeDtypeStruct((t, e), BF16),
        scratch_shapes=[pltpu.VMEM(hk, F32), pltpu.VMEM(hk, F32)],
        compiler_params=_params("parallel", "arbitrary"),
        name="peer_gate",
    )(hn, u, a1, a2, r)


def _mm_acc_body(a_ref, w_ref, o_ref):
    @pl.when(pl.program_id(2) == 0)
    def _():
        o_ref[...] = jnp.zeros_like(o_ref)

    o_ref[...] += jnp.dot(a_ref[...], w_ref[...], preferred_element_type=F32)


def mm_acc_call(a, w, tm=1024, tn=2048, tk=1024, name="mm_acc"):
    m, k = a.shape
    n = w.shape[1]
    tm, tn, tk = min(tm, m), min(tn, n), min(tk, k)
    return pl.pallas_call(
        _mm_acc_body,
        grid=(m // tm, n // tn, k // tk),
        in_specs=[pl.BlockSpec((tm, tk), lambda i, j, l: (i, l)), pl.BlockSpec((tk, tn), lambda i, j, l: (l, j))],
        out_specs=pl.BlockSpec((tm, tn), lambda i, j, l: (i, j)),
        out_shape=jax.ShapeDtypeStruct((m, n), F32),
        compiler_params=_params("parallel", "parallel", "arbitrary"),
        name=name,
    )(a, w)


def peer_ffn_call(h, norm_g, w_q, keys1, keys2, u, v):
    hn = rmsnorm_call(h, norm_g, BF16)
    q = mm_call(hn, w_q.astype(BF16), BF16, name="peer_q")
    a1, a2, r = peer_topk_call(q, keys1.astype(BF16), keys2.astype(BF16))
    w = peer_gate_call(hn, u.astype(BF16), a1, a2, r)
    return mm_acc_call(w, v.astype(BF16), name="peer_out")


def mixer_call(xn, h, w_in, w_out, lb, hgrn_norm, s5_tables, ssm_d, ssm_w_glu, ssm_norm, rel_bias, att_norm, bsz):
    n_a = 4 * D_HGRN + D_SSM
    proj_a = mm_call(xn, w_in[:, :n_a].astype(BF16), F32, name="in_proj_a")
    proj_c = mm_call(xn, w_in[:, n_a:].astype(BF16), BF16, name="in_proj_c")
    y_a = hgrn_call(proj_a, lb, hgrn_norm, bsz)
    y_b = s5_call(proj_a, 4, s5_tables, ssm_d, ssm_w_glu, ssm_norm, bsz)
    y_c = attention_call(proj_c, rel_bias, att_norm, bsz)
    mixed = jnp.concatenate([y_a, y_b, y_c], axis=-1)
    return mm_call(mixed, w_out.astype(BF16), F32, res=h, name="out_proj")


def kernel(x, w_in, w_out, hgrn_lb, hgrn_norm, ssm_a_re, ssm_a_im, ssm_log_dt, ssm_b_re, ssm_b_im, ssm_c_re, ssm_c_im, ssm_d, ssm_w_glu, ssm_norm, att_rel_bias, att_norm, norm_mix, norm_ffn, peer_w_q, peer_keys1, peer_keys2, peer_u, peer_v, norm_final):
    bsz, seq, d = x.shape
    lb_all = jnp.cumsum(jax.nn.softmax(hgrn_lb.astype(F32), axis=0), axis=0)
    lb_all = lb_all - lb_all[0]
    h = x.reshape(bsz * seq, d)
    xn = rmsnorm_call(h, norm_mix[0], BF16)
    out = None
    for layer in range(DEPTH):
        tables = _s5_prep(ssm_a_re[layer], ssm_a_im[layer], ssm_log_dt[layer], ssm_b_re[layer],
                          ssm_b_im[layer], ssm_c_re[layer], ssm_c_im[layer])
        h = mixer_call(xn, h, w_in[layer], w_out[layer], lb_all[layer], hgrn_norm[layer], tables,
                       ssm_d[layer], ssm_w_glu[layer], ssm_norm[layer], att_rel_bias[layer],
                       att_norm[layer], bsz)
        ffn = peer_ffn_call(h, norm_ffn[layer], peer_w_q[layer], peer_keys1[layer], peer_keys2[layer],
                            peer_u[layer], peer_v[layer])
        if layer + 1 < DEPTH:
            h, xn = add_rmsnorm_call(h, ffn, norm_mix[layer + 1], BF16, keep_sum=True)
        else:
            out = add_rmsnorm_call(h, ffn, norm_final, F32, keep_sum=False)
    return out.reshape(bsz, seq, d)
```

```python
import functools
import math

import jax
import jax.numpy as jnp
import numpy as np
from jax import lax
from jax.experimental import pallas as pl
from jax.experimental.pallas import tpu as pltpu

F32 = jnp.float32
BF16 = jnp.bfloat16

EPS = 1e-6
DEPTH = 2
CHUNK = 64
D_HGRN = 1024
HGRN_HEADS = 8
HGRN_HEAD_DIM = D_HGRN // HGRN_HEADS
HGRN_TILE = 128
D_SSM = 1024
SSM_GROUP = 16
SSM_GROUPS = D_SSM // SSM_GROUP
SSM_STATE = 64
SSM_CHUNK = 16
D_ATT = 2048
ATT_HEADS = 16
ATT_HEAD_DIM = D_ATT // ATT_HEADS
LEFT_CHUNKS = 8
MAX_REL = 256
ATT_TQ = 256
ATT_WIN = ATT_TQ + LEFT_CHUNKS * CHUNK
PEER_HEADS = 8
PEER_KEYS = 128
PEER_QDIM = 256
PEER_TOPK = 16

VMEM_LIMIT = 56 * 1024 * 1024

NT_DIMS = (((1,), (1,)), ((), ()))
TN_DIMS = (((0,), (0,)), ((), ()))


def _params(*sem):
    return pltpu.CompilerParams(dimension_semantics=sem, vmem_limit_bytes=VMEM_LIMIT)


def _rms(x, g):
    ms = jnp.mean(x * x, axis=-1, keepdims=True)
    return x * lax.rsqrt(ms + EPS) * g


def _rmsnorm_body(x_ref, g_ref, o_ref):
    o_ref[...] = _rms(x_ref[...], g_ref[...]).astype(o_ref.dtype)


def _add_rmsnorm_body(a_ref, b_ref, g_ref, *out_refs):
    s = a_ref[...] + b_ref[...]
    if len(out_refs) == 2:
        out_refs[0][...] = s
    out_refs[-1][...] = _rms(s, g_ref[...]).astype(out_refs[-1].dtype)


def rmsnorm_call(x, g, out_dtype, tm=256):
    t, d = x.shape
    row = pl.BlockSpec((tm, d), lambda i: (i, 0))
    return pl.pallas_call(
        _rmsnorm_body,
        grid=(t // tm,),
        in_specs=[row, pl.BlockSpec((1, d), lambda i: (0, 0))],
        out_specs=row,
        out_shape=jax.ShapeDtypeStruct((t, d), out_dtype),
        compiler_params=_params("parallel"),
        name="rmsnorm",
    )(x, g.reshape(1, d).astype(F32))


def add_rmsnorm_call(a, b, g, out_dtype, keep_sum, tm=256):
    t, d = a.shape
    row = pl.BlockSpec((tm, d), lambda i: (i, 0))
    out_specs = [row, row] if keep_sum else row
    normed = jax.ShapeDtypeStruct((t, d), out_dtype)
    out_shape = [jax.ShapeDtypeStruct((t, d), F32), normed] if keep_sum else normed
    return pl.pallas_call(
        _add_rmsnorm_body,
        grid=(t // tm,),
        in_specs=[row, row, pl.BlockSpec((1, d), lambda i: (0, 0))],
        out_specs=out_specs,
        out_shape=out_shape,
        compiler_params=_params("parallel"),
        name="add_rmsnorm",
    )(a, b, g.reshape(1, d).astype(F32))


def _mm_body(a_ref, w_ref, o_ref):
    w = w_ref[...].astype(BF16)
    o_ref[...] = jnp.dot(a_ref[...], w, preferred_element_type=F32).astype(o_ref.dtype)


def _mm_res_body(a_ref, w_ref, r_ref, o_ref):
    acc = jnp.dot(a_ref[...], w_ref[...].astype(BF16), preferred_element_type=F32)
    o_ref[...] = (r_ref[...] + acc).astype(o_ref.dtype)


def mm_call(a, w, layer, col0, n, out_dtype, res=None, tm=1024, tn=512, name="mm"):
    m, k = a.shape
    tm, tn = min(tm, m), min(tn, n)
    c0 = col0 // tn
    assert col0 % tn == 0 and n % tn == 0
    in_specs = [pl.BlockSpec((tm, k), lambda i, j: (i, 0)),
                pl.BlockSpec((None, k, tn), lambda i, j: (layer, 0, c0 + j))]
    args = [a, w]
    body = _mm_body
    if res is not None:
        in_specs.append(pl.BlockSpec((tm, tn), lambda i, j: (i, j)))
        args.append(res)
        body = _mm_res_body
    return pl.pallas_call(
        body,
        grid=(m // tm, n // tn),
        in_specs=in_specs,
        out_specs=pl.BlockSpec((tm, tn), lambda i, j: (i, j)),
        out_shape=jax.ShapeDtypeStruct((m, n), out_dtype),
        compiler_params=_params("parallel", "arbitrary"),
        name=name,
    )(*args)


def _hgrn_consts():
    c = HGRN_TILE
    r = np.arange(c)
    t, i = r[:, None], r[None, :]
    mats_q, mats_k, masks = [], [], []
    w = c // 2
    while w >= 1:
        seg = (r // (2 * w)) * (2 * w)
        second = (r % (2 * w)) >= w
        mats_q.append(second[:, None] & (i >= (seg + w)[:, None]) & (i <= t))
        mats_k.append(~second[:, None] & (i >= t + 1) & (i <= (seg + w - 1)[:, None]))
        masks.append((seg[:, None] == seg[None, :]) & second[:, None] & ~second[None, :])
        w //= 2
    masks.append(t == i)
    mats = [i <= t, i > t] + mats_q + mats_k
    return (jnp.asarray(np.stack(mats).astype(np.float32), BF16),
            jnp.asarray(np.stack(masks).astype(np.float32), F32), len(mats_q))


def _hgrn_body(q_ref, f_ref, i_ref, g_ref, am_ref, mask_ref, prm_ref, o_ref,
               st_ref, qt_ref, kt_ref, *, levels):
    hd = HGRN_HEAD_DIM

    @pl.when(pl.program_id(1) == 0)
    def _():
        st_ref[...] = jnp.zeros_like(st_ref)

    z = f_ref[...]
    log_lb, log_1m_lb, one_m_lb, norm_g = prm_ref[0:1], prm_ref[1:2], prm_ref[2:3], prm_ref[3:4]
    e = jnp.exp(-jnp.abs(z))
    log_sig = jnp.minimum(z, 0.0) - jnp.log1p(e)
    cc = log_1m_lb + log_sig
    log_f = jnp.maximum(log_lb, cc) + jnp.log1p(jnp.exp(-jnp.abs(log_lb - cc)))
    k = one_m_lb * jnp.where(z >= 0.0, e, 1.0) / (1.0 + e)
    q = q_ref[...]
    lf_hi = log_f.astype(BF16)
    lf_lo = (log_f - lf_hi.astype(F32)).astype(BF16)

    def exponent(m):
        a = am_ref[m]
        return jnp.dot(a, lf_hi, preferred_element_type=F32) + jnp.dot(a, lf_lo, preferred_element_type=F32)

    eb = jnp.exp(exponent(0))
    decay_all = eb[HGRN_TILE - 1:HGRN_TILE, :]
    qt_ref[0] = (q * eb).astype(BF16)
    kt_ref[0] = (k * jnp.exp(exponent(1))).astype(BF16)
    for lv in range(levels):
        qt_ref[1 + lv] = (q * jnp.exp(exponent(2 + lv))).astype(BF16)
        kt_ref[1 + lv] = (k * jnp.exp(exponent(2 + levels + lv))).astype(BF16)
    qt_ref[1 + levels] = q.astype(BF16)
    kt_ref[1 + levels] = k.astype(BF16)

    for h in range(HGRN_HEADS):
        hs = slice(h * hd, (h + 1) * hd)
        scores = None
        for lv in range(levels + 1):
            sc = lax.dot_general(qt_ref[1 + lv, :, hs], kt_ref[1 + lv, :, hs], NT_DIMS,
                                 preferred_element_type=F32) * mask_ref[lv]
            scores = sc if scores is None else scores + sc
        v = i_ref[:, hs].astype(BF16)
        st = st_ref[h]
        o = jnp.dot(scores.astype(BF16), v, preferred_element_type=F32)
        o = o + lax.dot_general(qt_ref[0, :, hs], st.astype(BF16), NT_DIMS, preferred_element_type=F32)
        st_ref[h] = st * decay_all[:, hs] + lax.dot_general(v, kt_ref[0, :, hs], TN_DIMS,
                                                            preferred_element_type=F32)
        o = o * lax.rsqrt(jnp.mean(o * o, axis=-1, keepdims=True) + EPS) * norm_g[:, hs]
        o_ref[:, hs] = (o * jax.nn.silu(g_ref[:, hs])).astype(o_ref.dtype)


def hgrn_call(proj, lb, norm_g, bsz):
    t = proj.shape[0]
    nblk = t // bsz // HGRN_TILE
    am, masks, levels = _hgrn_consts()
    lb = lb.astype(F32)
    prm = jnp.stack([jnp.log(lb), jnp.log1p(-lb), 1.0 - lb, norm_g.astype(F32)])
    col = lambda c: pl.BlockSpec((HGRN_TILE, D_HGRN), lambda b, n: (b * nblk + n, c))
    whole = lambda a: pl.BlockSpec(a.shape, lambda b, n: (0,) * a.ndim)
    return pl.pallas_call(
        functools.partial(_hgrn_body, levels=levels),
        grid=(bsz, nblk),
        in_specs=[col(0), col(1), col(2), col(3), whole(am), whole(masks), whole(prm)],
        out_specs=pl.BlockSpec((HGRN_TILE, D_HGRN), lambda b, n: (b * nblk + n, 0)),
        out_shape=jax.ShapeDtypeStruct((t, D_HGRN), BF16),
        scratch_shapes=[pltpu.VMEM((HGRN_HEADS, HGRN_HEAD_DIM, HGRN_HEAD_DIM), F32),
                        pltpu.VMEM((levels + 2, HGRN_TILE, D_HGRN), BF16),
                        pltpu.VMEM((levels + 2, HGRN_TILE, D_HGRN), BF16)],
        compiler_params=_params("parallel", "arbitrary"),
        name="hgrn2",
    )(proj, proj, proj, proj, am, masks, prm)


def _s5_prep(a_re, a_im, log_dt, b_re, b_im, c_re, c_im):
    ln, hp = SSM_CHUNK, lax.Precision.HIGHEST
    a = lax.complex(a_re.astype(F32), a_im.astype(F32))
    adt = a * jnp.exp(log_dt.astype(F32))[:, None]
    a_bar = jnp.exp(adt)
    b_bar = ((a_bar - 1.0) / a)[..., None] * lax.complex(b_re.astype(F32), b_im.astype(F32))
    c_mat = lax.complex(c_re.astype(F32), c_im.astype(F32))
    steps = jnp.arange(ln + 1, dtype=F32)
    pw = jnp.exp(adt[:, None, :] * steps[None, :, None])
    g, p, m = b_bar.shape
    kern = jnp.einsum('gmp,gtp,gpn->gtmn', c_mat, pw[:, :ln], b_bar, precision=hp).real
    lag = np.arange(ln)[:, None] - np.arange(ln)[None, :]
    place = jnp.asarray((lag[:, :, None] == np.arange(ln)[None, None, :]).astype(np.float32))
    toe = jnp.einsum('abt,gtmn->gabmn', place, kern, precision=hp)
    tg_t = toe.transpose(0, 2, 4, 1, 3).reshape(g, ln * m, ln * m)
    inc = pw[:, ln - 1::-1][:, :ln, None, :] * b_bar.transpose(0, 2, 1)[:, None, :, :]
    inc = inc.reshape(g, ln * m, p)
    m_b = jnp.concatenate([inc.real, inc.imag], axis=-1)
    out = c_mat.transpose(0, 2, 1)[:, :, None, :] * pw[:, 1:ln + 1].transpose(0, 2, 1)[:, :, :, None]
    out = out.reshape(g, p, ln * m)
    m_c = jnp.concatenate([out.real, -out.imag], axis=1)
    cp = jnp.exp(adt[:, None, :] * (ln * jnp.arange(9, dtype=F32))[None, :, None])
    rr = jnp.concatenate([cp.real, cp.real], axis=-1)
    ii = jnp.concatenate([-cp.imag, cp.imag], axis=-1)
    sel = np.array([1, 2, 4, 8])
    return (tg_t.astype(BF16), m_b.astype(BF16), m_c.astype(BF16),
            rr[:, :8], ii[:, :8], rr[:, sel], ii[:, sel])


def _s5_body(u_ref, tg_ref, mb_ref, mc_ref, prr_ref, pii_ref, srr_ref, sii_ref, y_ref, d_ref, x_ref, *, chains):
    u = u_ref[0]
    d_ref[...] = jnp.dot(u, mb_ref[0], preferred_element_type=F32)
    half = SSM_STATE

    def cmul(rr, ii, zz):
        return rr * zz + ii * pltpu.roll(zz, half, axis=1)

    row = lax.broadcasted_iota(jnp.int32, (8, 2 * half), 0)
    prr, pii, srr, sii = prr_ref[0], pii_ref[0], srr_ref[0], sii_ref[0]
    tiles = d_ref.shape[0] // 8 // chains
    for b in range(chains):
        carry = jnp.zeros((1, 2 * half), F32)
        for tl in range(tiles):
            r0 = (b * tiles + tl) * 8
            pre = d_ref[r0:r0 + 8, :]
            for n, s in enumerate((1, 2, 4)):
                shifted = jnp.where(row >= s, pltpu.roll(pre, s, axis=0), 0.0)
                pre = pre + cmul(srr[n:n + 1], sii[n:n + 1], shifted)
            start = cmul(prr, pii, jnp.broadcast_to(carry, pre.shape))
            x_ref[r0:r0 + 8, :] = start + jnp.where(row >= 1, pltpu.roll(pre, 1, axis=0), 0.0)
            carry = cmul(srr[3:4], sii[3:4], carry) + pre[7:8, :]
    y = jnp.dot(u, tg_ref[0], preferred_element_type=F32)
    y_ref[0] = y + jnp.dot(x_ref[...].astype(BF16), mc_ref[0], preferred_element_type=F32)


def _s5_post_body(y_ref, u_ref, prm_ref, w_ref, o_ref):
    y = y_ref[...] + prm_ref[0:1] * u_ref[...]
    y = jax.nn.gelu(y, approximate=True)
    gate = jnp.dot(y.astype(BF16), w_ref[...], preferred_element_type=F32)
    o_ref[...] = _rms(y * jax.nn.sigmoid(gate), prm_ref[1:2]).astype(o_ref.dtype)


def s5_call(proj, ucol, tables, d_skip, w_glu, norm_g, bsz, tm=512):
    t = proj.shape[0]
    rows, width = t // SSM_CHUNK, SSM_CHUNK * SSM_GROUP
    u = proj[:, ucol * D_SSM:(ucol + 1) * D_SSM]
    ug = u.reshape(rows, SSM_CHUNK, SSM_GROUPS, SSM_GROUP).transpose(2, 0, 1, 3).reshape(SSM_GROUPS, rows, width)
    grp = lambda a: pl.BlockSpec((1,) + a.shape[1:], lambda g: (g,) + (0,) * (a.ndim - 1))
    args = (ug.astype(BF16),) + tuple(tables)
    y = pl.pallas_call(
        functools.partial(_s5_body, chains=bsz),
        grid=(SSM_GROUPS,),
        in_specs=[grp(a) for a in args],
        out_specs=pl.BlockSpec((1, rows, width), lambda g: (g, 0, 0)),
        out_shape=jax.ShapeDtypeStruct((SSM_GROUPS, rows, width), F32),
        scratch_shapes=[pltpu.VMEM((rows, 2 * SSM_STATE), F32), pltpu.VMEM((rows, 2 * SSM_STATE), F32)],
        compiler_params=_params("parallel"),
        name="s5_scan",
    )(*args)
    y = y.reshape(SSM_GROUPS, rows, SSM_CHUNK, SSM_GROUP).transpose(1, 2, 0, 3).reshape(t, D_SSM)
    tm = min(tm, t)
    prm = jnp.stack([d_skip.astype(F32), norm_g.astype(F32)])
    return pl.pallas_call(
        _s5_post_body,
        grid=(t // tm,),
        in_specs=[pl.BlockSpec((tm, D_SSM), lambda i: (i, 0)),
                  pl.BlockSpec((tm, D_SSM), lambda i: (i, ucol)),
                  pl.BlockSpec((2, D_SSM), lambda i: (0, 0)),
                  pl.BlockSpec((D_SSM, D_SSM), lambda i: (0, 0))],
        out_specs=pl.BlockSpec((tm, D_SSM), lambda i: (i, 0)),
        out_shape=jax.ShapeDtypeStruct((t, D_SSM), BF16),
        compiler_params=_params("parallel"),
        name="s5_post",
    )(y, proj, prm, w_glu.astype(BF16))


def _att_bias(rel_bias):
    qi = np.arange(ATT_TQ)[:, None]
    kj = np.arange(ATT_WIN)[None, :]
    back = kj // CHUNK - qi // CHUNK
    valid = (back >= 0) & (back <= LEFT_CHUNKS)
    rb = rel_bias.astype(F32)
    nh = rb.shape[0]
    shift = LEFT_CHUNKS * CHUNK
    period = ATT_TQ + ATT_WIN - 1
    n_mid = ATT_WIN - (shift - MAX_REL)
    assert shift >= MAX_REL and n_mid <= 2 * MAX_REL + 1
    desc = rb[:, ::-1]
    far = lambda n: jnp.broadcast_to(desc[:, :1], (nh, n))
    g = jnp.concatenate([far(shift - MAX_REL), desc[:, :n_mid], far(ATT_TQ - 1)], axis=1)
    flat = jnp.tile(g, (1, ATT_TQ))[:, :ATT_TQ * (period - 1)]
    toeplitz = flat.reshape(nh, ATT_TQ, period - 1)[:, :, :ATT_WIN]
    return jnp.where(valid[None], toeplitz, -1e30)


def _att_body(q_ref, k_ref, v_ref, b_ref, o_ref):
    scale = ATT_HEAD_DIM ** -0.5
    lead = ATT_WIN // ATT_TQ - 1
    for i in range(q_ref.shape[1] // ATT_TQ):
        q0 = i * ATT_TQ
        k0 = max(0, q0 - (ATT_WIN - ATT_TQ))
        nk = q0 + ATT_TQ - k0
        bias = b_ref[0] if i >= lead else b_ref[0, :, ATT_WIN - nk:]
        s = lax.dot_general(q_ref[0, q0:q0 + ATT_TQ, :], k_ref[0, k0:k0 + nk, :], NT_DIMS,
                            preferred_element_type=F32) * scale + bias
        p = jnp.exp(s - jnp.max(s, axis=-1, keepdims=True))
        l = jnp.sum(p, axis=-1, keepdims=True)
        o_ref[0, q0:q0 + ATT_TQ, :] = jnp.dot(p.astype(BF16), v_ref[0, k0:k0 + nk, :],
                                              preferred_element_type=F32) / l


def attention_call(qkv, rel_bias, norm_g, bsz):
    t = qkv.shape[0]
    seq = t // bsz
    x = qkv.reshape(bsz, seq, 3 * D_ATT)
    bias = _att_bias(rel_bias)
    head = lambda c: pl.BlockSpec((1, seq, ATT_HEAD_DIM), lambda b, h: (b, 0, c * ATT_HEADS + h))
    o = pl.pallas_call(
        _att_body,
        grid=(bsz, ATT_HEADS),
        in_specs=[head(0), head(1), head(2), pl.BlockSpec((1, ATT_TQ, ATT_WIN), lambda b, h: (h, 0, 0))],
        out_specs=head(0),
        out_shape=jax.ShapeDtypeStruct((bsz, seq, D_ATT), F32),
        compiler_params=_params("parallel", "parallel"),
        name="chunk_attention",
    )(x, x, x, bias)
    return rmsnorm_call(o.reshape(t, D_ATT), norm_g, BF16)


def _top_rows(x, k):
    rows = []
    idx = lax.broadcasted_iota(jnp.int32, x.shape, 0).astype(F32)
    for i in range(k):
        m = jnp.max(x, axis=0, keepdims=True)
        rows.append(m)
        if i + 1 < k:
            first = jnp.min(jnp.where(x == m, idx, float(x.shape[0])), axis=0, keepdims=True)
            x = jnp.where(idx == first, -jnp.inf, x)
    return rows


def _peer_topk_body(q_ref, k1_ref, k2_ref, a1_ref, a2_ref, r_ref):
    half = PEER_QDIM // 2
    for h in range(PEER_HEADS):
        base = h * PEER_QDIM
        s1 = lax.dot_general(k1_ref[...], q_ref[:, base:base + half], NT_DIMS, preferred_element_type=F32)
        s2 = lax.dot_general(k2_ref[...], q_ref[:, base + half:base + PEER_QDIM], NT_DIMS,
                             preferred_element_type=F32)
        a1_ref[h] = s1
        a2_ref[h] = s2
        v1 = _top_rows(s1, PEER_TOPK)
        v2 = jnp.concatenate(_top_rows(s2, PEER_TOPK), axis=0)
        cand = jnp.concatenate([v1[a] + v2[:PEER_TOPK // (a + 1)] for a in range(PEER_TOPK)], axis=0)
        top = _top_rows(cand, PEER_TOPK)
        z = jnp.ones_like(top[0])
        for t in top[1:]:
            z = z + jnp.exp(t - top[0])
        r_ref[0, h:h + 1, :] = top[PEER_TOPK - 1]
        r_ref[1, h:h + 1, :] = v1[0]
        r_ref[2, h:h + 1, :] = v2[0:1]
        r_ref[3, h:h + 1, :] = 1.0 / z


def peer_topk_call(q, keys1, keys2, tm=256):
    t = q.shape[0]
    hk = (PEER_HEADS, PEER_KEYS, t)
    return pl.pallas_call(
        _peer_topk_body,
        grid=(t // tm,),
        in_specs=[pl.BlockSpec((tm, PEER_HEADS * PEER_QDIM), lambda i: (i, 0)),
                  pl.BlockSpec((PEER_KEYS, PEER_QDIM // 2), lambda i: (0, 0)),
                  pl.BlockSpec((PEER_KEYS, PEER_QDIM // 2), lambda i: (0, 0))],
        out_specs=[pl.BlockSpec((PEER_HEADS, PEER_KEYS, tm), lambda i: (0, 0, i)),
                   pl.BlockSpec((PEER_HEADS, PEER_KEYS, tm), lambda i: (0, 0, i)),
                   pl.BlockSpec((4, PEER_HEADS, tm), lambda i: (0, 0, i))],
        out_shape=[jax.ShapeDtypeStruct(hk, F32), jax.ShapeDtypeStruct(hk, F32),
                   jax.ShapeDtypeStruct((4, PEER_HEADS, t), F32)],
        compiler_params=_params("parallel"),
        name="peer_topk",
    )(q, keys1, keys2)


def _peer_gate_body(x_ref, u_ref, a1_ref, a2_ref, r_ref, w_ref, e2_ref, c1_ref, *, te):
    j = pl.program_id(1)

    @pl.when(j == 0)
    def _():
        for h in range(PEER_HEADS):
            e2_ref[h] = jnp.exp(a2_ref[h] - r_ref[2, h:h + 1, :])
            c1_ref[h] = jnp.exp(a1_ref[h] - r_ref[1, h:h + 1, :]) * r_ref[3, h:h + 1, :]

    ht = lax.dot_general(u_ref[...], x_ref[...], NT_DIMS, preferred_element_type=F32)
    for s in range(te // PEER_KEYS):
        i1 = j * (te // PEER_KEYS) + s
        gate = None
        for h in range(PEER_HEADS):
            sums = a1_ref[h, pl.ds(i1, 1), :] + a2_ref[h]
            term = jnp.where(sums >= r_ref[0, h:h + 1, :], e2_ref[h], 0.0) * c1_ref[h, pl.ds(i1, 1), :]
            gate = term if gate is None else gate + term
        act = jax.nn.gelu(ht[s * PEER_KEYS:(s + 1) * PEER_KEYS], approximate=True)
        w_ref[:, s * PEER_KEYS:(s + 1) * PEER_KEYS] = jnp.transpose(gate * act).astype(w_ref.dtype)


def peer_gate_call(hn, u, a1, a2, r, tm=512, te=512):
    t, d = hn.shape
    e = u.shape[0]
    tm = min(tm, t)
    hk = (PEER_HEADS, PEER_KEYS, tm)
    return pl.pallas_call(
        functools.partial(_peer_gate_body, te=te),
        grid=(t // tm, e // te),
        in_specs=[pl.BlockSpec((tm, d), lambda i, j: (i, 0)),
                  pl.BlockSpec((te, d), lambda i, j: (j, 0)),
                  pl.BlockSpec(hk, lambda i, j: (0, 0, i)),
                  pl.BlockSpec(hk, lambda i, j: (0, 0, i)),
                  pl.BlockSpec((4, PEER_HEADS, tm), lambda i, j: (0, 0, i))],
        out_specs=pl.BlockSpec((tm, te), lambda i, j: (i, j)),
        out_shape=jax.ShapeDtypeStruct((t, e), BF16),
        scratch_shapes=[pltpu.VMEM(hk, F32), pltpu.VMEM(hk, F32)],
        compiler_params=_params("parallel", "arbitrary"),
        name="peer_gate",
    )(hn, u, a1, a2, r)


def _mm_acc_body(a_ref, w_ref, o_ref):
    @pl.when(pl.program_id(2) == 0)
    def _():
        o_ref[...] = jnp.zeros_like(o_ref)

    o_ref[...] += jnp.dot(a_ref[...], w_ref[...], preferred_element_type=F32)


def mm_acc_call(a, w, tm=1024, tn=2048, tk=1024, name="mm_acc"):
    m, k = a.shape
    n = w.shape[1]
    tm, tn, tk = min(tm, m), min(tn, n), min(tk, k)
    return pl.pallas_call(
        _mm_acc_body,
        grid=(m // tm, n // tn, k // tk),
        in_specs=[pl.BlockSpec((tm, tk), lambda i, j, l: (i, l)), pl.BlockSpec((tk, tn), lambda i, j, l: (l, j))],
        out_specs=pl.BlockSpec((tm, tn), lambda i, j, l: (i, j)),
        out_shape=jax.ShapeDtypeStruct((m, n), F32),
        compiler_params=_params("parallel", "parallel", "arbitrary"),
        name=name,
    )(a, w)


def peer_ffn_call(h, norm_g, w_q, layer, keys1, keys2, u, v):
    hn = rmsnorm_call(h, norm_g, BF16)
    q = mm_call(hn, w_q, layer, 0, w_q.shape[-1], BF16, name="peer_q")
    a1, a2, r = peer_topk_call(q, keys1.astype(BF16), keys2.astype(BF16))
    w = peer_gate_call(hn, u.astype(BF16), a1, a2, r)
    return mm_acc_call(w, v.astype(BF16), name="peer_out")


def mixer_call(xn, h, w_in, w_out, layer, lb, hgrn_norm, s5_tables, ssm_d, ssm_w_glu, ssm_norm, rel_bias, att_norm,
               bsz):
    n_a = 4 * D_HGRN + D_SSM
    proj_a = mm_call(xn, w_in, layer, 0, n_a, F32, name="in_proj_a")
    proj_c = mm_call(xn, w_in, layer, n_a, 3 * D_ATT, BF16, name="in_proj_c")
    y_a = hgrn_call(proj_a, lb, hgrn_norm, bsz)
    y_b = s5_call(proj_a, 4, s5_tables, ssm_d, ssm_w_glu, ssm_norm, bsz)
    y_c = attention_call(proj_c, rel_bias, att_norm, bsz)
    mixed = jnp.concatenate([y_a, y_b, y_c], axis=-1)
    return mm_call(mixed, w_out, layer, 0, w_out.shape[-1], F32, res=h, name="out_proj")


def kernel(x, w_in, w_out, hgrn_lb, hgrn_norm, ssm_a_re, ssm_a_im, ssm_log_dt, ssm_b_re, ssm_b_im, ssm_c_re, ssm_c_im, ssm_d, ssm_w_glu, ssm_norm, att_rel_bias, att_norm, norm_mix, norm_ffn, peer_w_q, peer_keys1, peer_keys2, peer_u, peer_v, norm_final):
    bsz, seq, d = x.shape
    lb_all = jnp.cumsum(jax.nn.softmax(hgrn_lb.astype(F32), axis=0), axis=0)
    lb_all = lb_all - lb_all[0]
    h = x.reshape(bsz * seq, d)
    xn = rmsnorm_call(h, norm_mix[0], BF16)
    out = None
    for layer in range(DEPTH):
        tables = _s5_prep(ssm_a_re[layer], ssm_a_im[layer], ssm_log_dt[layer], ssm_b_re[layer],
                          ssm_b_im[layer], ssm_c_re[layer], ssm_c_im[layer])
        h = mixer_call(xn, h, w_in, w_out, layer, lb_all[layer], hgrn_norm[layer], tables,
                       ssm_d[layer], ssm_w_glu[layer], ssm_norm[layer], att_rel_bias[layer],
                       att_norm[layer], bsz)
        ffn = peer_ffn_call(h, norm_ffn[layer], peer_w_q, layer, peer_keys1[layer], peer_keys2[layer],
                            peer_u[layer], peer_v[layer])
        if layer + 1 < DEPTH:
            h, xn = add_rmsnorm_call(h, ffn, norm_mix[layer + 1], BF16, keep_sum=True)
        else:
            out = add_rmsnorm_call(h, ffn, norm_final, F32, keep_sum=False)
    return out.reshape(bsz, seq, d)
```

```python
import functools
import math

import jax
import jax.numpy as jnp
import numpy as np
from jax import lax
from jax.experimental import pallas as pl
from jax.experimental.pallas import tpu as pltpu

F32 = jnp.float32
BF16 = jnp.bfloat16

EPS = 1e-6
DEPTH = 2
CHUNK = 64
D_HGRN = 1024
HGRN_HEADS = 8
HGRN_HEAD_DIM = D_HGRN // HGRN_HEADS
HGRN_TILE = 128
D_SSM = 1024
SSM_GROUP = 16
SSM_GROUPS = D_SSM // SSM_GROUP
SSM_STATE = 64
SSM_CHUNK = 16
SSM_GB = 8
D_ATT = 2048
ATT_HEADS = 16
ATT_HEAD_DIM = D_ATT // ATT_HEADS
LEFT_CHUNKS = 8
MAX_REL = 256
ATT_TQ = 256
ATT_WIN = ATT_TQ + LEFT_CHUNKS * CHUNK
PEER_HEADS = 8
PEER_KEYS = 128
PEER_QDIM = 256
PEER_TOPK = 16

VMEM_LIMIT = 56 * 1024 * 1024

NT_DIMS = (((1,), (1,)), ((), ()))
TN_DIMS = (((0,), (0,)), ((), ()))


def _params(*sem):
    return pltpu.CompilerParams(dimension_semantics=sem, vmem_limit_bytes=VMEM_LIMIT)


def _rms(x, g):
    ms = jnp.mean(x * x, axis=-1, keepdims=True)
    return x * lax.rsqrt(ms + EPS) * g


def _rmsnorm_body(x_ref, g_ref, o_ref):
    o_ref[...] = _rms(x_ref[...], g_ref[...]).astype(o_ref.dtype)


def _add_rmsnorm_body(a_ref, b_ref, g_ref, *out_refs):
    s = a_ref[...] + b_ref[...]
    if len(out_refs) == 2:
        out_refs[0][...] = s
    out_refs[-1][...] = _rms(s, g_ref[...]).astype(out_refs[-1].dtype)


def rmsnorm_call(x, g, out_dtype, tm=256):
    t, d = x.shape
    row = pl.BlockSpec((tm, d), lambda i: (i, 0))
    return pl.pallas_call(
        _rmsnorm_body,
        grid=(t // tm,),
        in_specs=[row, pl.BlockSpec((1, d), lambda i: (0, 0))],
        out_specs=row,
        out_shape=jax.ShapeDtypeStruct((t, d), out_dtype),
        compiler_params=_params("parallel"),
        name="rmsnorm",
    )(x, g.reshape(1, d).astype(F32))


def add_rmsnorm_call(a, b, g, out_dtype, keep_sum, tm=256):
    t, d = a.shape
    row = pl.BlockSpec((tm, d), lambda i: (i, 0))
    out_specs = [row, row] if keep_sum else row
    normed = jax.ShapeDtypeStruct((t, d), out_dtype)
    out_shape = [jax.ShapeDtypeStruct((t, d), F32), normed] if keep_sum else normed
    return pl.pallas_call(
        _add_rmsnorm_body,
        grid=(t // tm,),
        in_specs=[row, row, pl.BlockSpec((1, d), lambda i: (0, 0))],
        out_specs=out_specs,
        out_shape=out_shape,
        compiler_params=_params("parallel"),
        name="add_rmsnorm",
    )(a, b, g.reshape(1, d).astype(F32))


def _mm_body(a_ref, w_ref, o_ref):
    w = w_ref[...].astype(BF16)
    o_ref[...] = jnp.dot(a_ref[...], w, preferred_element_type=F32).astype(o_ref.dtype)


def _mm_res_body(a_ref, w_ref, r_ref, o_ref):
    acc = jnp.dot(a_ref[...], w_ref[...].astype(BF16), preferred_element_type=F32)
    o_ref[...] = (r_ref[...] + acc).astype(o_ref.dtype)


def mm_call(a, w, layer, col0, n, out_dtype, res=None, tm=1024, tn=512, name="mm"):
    m, k = a.shape
    tm, tn = min(tm, m), min(tn, n)
    c0 = col0 // tn
    assert col0 % tn == 0 and n % tn == 0
    in_specs = [pl.BlockSpec((tm, k), lambda i, j: (i, 0)),
                pl.BlockSpec((None, k, tn), lambda i, j: (layer, 0, c0 + j))]
    args = [a, w]
    body = _mm_body
    if res is not None:
        in_specs.append(pl.BlockSpec((tm, tn), lambda i, j: (i, j)))
        args.append(res)
        body = _mm_res_body
    return pl.pallas_call(
        body,
        grid=(m // tm, n // tn),
        in_specs=in_specs,
        out_specs=pl.BlockSpec((tm, tn), lambda i, j: (i, j)),
        out_shape=jax.ShapeDtypeStruct((m, n), out_dtype),
        compiler_params=_params("parallel", "arbitrary"),
        name=name,
    )(*args)


def _hgrn_consts():
    c = HGRN_TILE
    r = np.arange(c)
    t, i = r[:, None], r[None, :]
    mats_q, mats_k, masks = [], [], []
    w = c // 2
    while w >= 1:
        seg = (r // (2 * w)) * (2 * w)
        second = (r % (2 * w)) >= w
        mats_q.append(second[:, None] & (i >= (seg + w)[:, None]) & (i <= t))
        mats_k.append(~second[:, None] & (i >= t + 1) & (i <= (seg + w - 1)[:, None]))
        masks.append((seg[:, None] == seg[None, :]) & second[:, None] & ~second[None, :])
        w //= 2
    masks.append(t == i)
    mats = [i <= t, i > t] + mats_q + mats_k
    return (jnp.asarray(np.stack(mats).astype(np.float32), BF16),
            jnp.asarray(np.stack(masks).astype(np.float32), F32), len(mats_q))


def _hgrn_body(q_ref, f_ref, i_ref, g_ref, am_ref, mask_ref, prm_ref, o_ref,
               st_ref, qt_ref, kt_ref, *, levels):
    hd = HGRN_HEAD_DIM

    @pl.when(pl.program_id(1) == 0)
    def _():
        st_ref[...] = jnp.zeros_like(st_ref)

    z = f_ref[...]
    log_lb, log_1m_lb, one_m_lb, norm_g = prm_ref[0:1], prm_ref[1:2], prm_ref[2:3], prm_ref[3:4]
    e = jnp.exp(-jnp.abs(z))
    log_sig = jnp.minimum(z, 0.0) - jnp.log1p(e)
    cc = log_1m_lb + log_sig
    log_f = jnp.maximum(log_lb, cc) + jnp.log1p(jnp.exp(-jnp.abs(log_lb - cc)))
    k = one_m_lb * jnp.where(z >= 0.0, e, 1.0) / (1.0 + e)
    q = q_ref[...]
    lf_hi = log_f.astype(BF16)
    lf_lo = (log_f - lf_hi.astype(F32)).astype(BF16)

    def exponent(m):
        a = am_ref[m]
        return jnp.dot(a, lf_hi, preferred_element_type=F32) + jnp.dot(a, lf_lo, preferred_element_type=F32)

    eb = jnp.exp(exponent(0))
    decay_all = eb[HGRN_TILE - 1:HGRN_TILE, :]
    qt_ref[0] = (q * eb).astype(BF16)
    kt_ref[0] = (k * jnp.exp(exponent(1))).astype(BF16)
    for lv in range(levels):
        qt_ref[1 + lv] = (q * jnp.exp(exponent(2 + lv))).astype(BF16)
        kt_ref[1 + lv] = (k * jnp.exp(exponent(2 + levels + lv))).astype(BF16)
    qt_ref[1 + levels] = q.astype(BF16)
    kt_ref[1 + levels] = k.astype(BF16)

    for h in range(HGRN_HEADS):
        hs = slice(h * hd, (h + 1) * hd)
        scores = None
        for lv in range(levels + 1):
            sc = lax.dot_general(qt_ref[1 + lv, :, hs], kt_ref[1 + lv, :, hs], NT_DIMS,
                                 preferred_element_type=F32) * mask_ref[lv]
            scores = sc if scores is None else scores + sc
        v = i_ref[:, hs].astype(BF16)
        st = st_ref[h]
        o = jnp.dot(scores.astype(BF16), v, preferred_element_type=F32)
        o = o + lax.dot_general(qt_ref[0, :, hs], st.astype(BF16), NT_DIMS, preferred_element_type=F32)
        st_ref[h] = st * decay_all[:, hs] + lax.dot_general(v, kt_ref[0, :, hs], TN_DIMS,
                                                            preferred_element_type=F32)
        o = o * lax.rsqrt(jnp.mean(o * o, axis=-1, keepdims=True) + EPS) * norm_g[:, hs]
        o_ref[:, hs] = (o * jax.nn.silu(g_ref[:, hs])).astype(o_ref.dtype)


def hgrn_call(proj, lb, norm_g, bsz):
    t = proj.shape[0]
    nblk = t // bsz // HGRN_TILE
    am, masks, levels = _hgrn_consts()
    lb = lb.astype(F32)
    prm = jnp.stack([jnp.log(lb), jnp.log1p(-lb), 1.0 - lb, norm_g.astype(F32)])
    col = lambda c: pl.BlockSpec((HGRN_TILE, D_HGRN), lambda b, n: (b * nblk + n, c))
    whole = lambda a: pl.BlockSpec(a.shape, lambda b, n: (0,) * a.ndim)
    return pl.pallas_call(
        functools.partial(_hgrn_body, levels=levels),
        grid=(bsz, nblk),
        in_specs=[col(0), col(1), col(2), col(3), whole(am), whole(masks), whole(prm)],
        out_specs=pl.BlockSpec((HGRN_TILE, D_HGRN), lambda b, n: (b * nblk + n, 0)),
        out_shape=jax.ShapeDtypeStruct((t, D_HGRN), BF16),
        scratch_shapes=[pltpu.VMEM((HGRN_HEADS, HGRN_HEAD_DIM, HGRN_HEAD_DIM), F32),
                        pltpu.VMEM((levels + 2, HGRN_TILE, D_HGRN), BF16),
                        pltpu.VMEM((levels + 2, HGRN_TILE, D_HGRN), BF16)],
        compiler_params=_params("parallel", "arbitrary"),
        name="hgrn2",
    )(proj, proj, proj, proj, am, masks, prm)


def _s5_prep(a_re, a_im, log_dt, b_re, b_im, c_re, c_im):
    ln, gb, hp = SSM_CHUNK, SSM_GB, lax.Precision.HIGHEST
    a = lax.complex(a_re.astype(F32), a_im.astype(F32))
    adt = a * jnp.exp(log_dt.astype(F32))[:, None]
    a_bar = jnp.exp(adt)
    b_bar = ((a_bar - 1.0) / a)[..., None] * lax.complex(b_re.astype(F32), b_im.astype(F32))
    c_mat = lax.complex(c_re.astype(F32), c_im.astype(F32))
    steps = jnp.arange(ln + 1, dtype=F32)
    pw = jnp.exp(adt[:, None, :] * steps[None, :, None])
    g, p, m = b_bar.shape
    nb = g // gb
    eye = jnp.eye(gb, dtype=F32)
    kern = jnp.einsum('gmp,gtp,gpn->gtmn', c_mat, pw[:, :ln], b_bar, precision=hp).real
    lag = np.arange(ln)[:, None] - np.arange(ln)[None, :]
    place = jnp.asarray((lag[:, :, None] == np.arange(ln)[None, None, :]).astype(np.float32))
    toe = jnp.einsum('abt,gtmn->gabmn', place, kern, precision=hp)
    toe = toe.reshape(nb, gb, ln, ln, m, m).transpose(0, 3, 5, 2, 1, 4)
    big_t = (toe[:, :, None] * eye[None, None, :, None, None, :, None]).reshape(nb, ln * gb * m, ln * gb * m)
    inc = pw[:, ln - 1::-1][:, :ln, None, :] * b_bar.transpose(0, 2, 1)[:, None, :, :]
    inc = inc.reshape(nb, gb, ln, m, p).transpose(0, 2, 3, 1, 4)
    inc = (inc[:, :, None] * eye[None, None, :, None, :, None]).reshape(nb, ln * gb * m, gb * p)
    m_b = jnp.concatenate([inc.real, inc.imag], axis=-1)
    out = c_mat.transpose(0, 2, 1)[:, :, None, :] * pw[:, 1:ln + 1].transpose(0, 2, 1)[:, :, :, None]
    out = out.reshape(nb, gb, p, ln, m).transpose(0, 2, 3, 1, 4)
    out = (out[:, None] * eye[None, :, None, None, :, None]).reshape(nb, gb * p, ln * gb * m)
    m_c = jnp.concatenate([out.real, -out.imag], axis=1)
    cp = jnp.exp(adt[:, None, :] * (ln * jnp.arange(9, dtype=F32))[None, :, None])
    cp = cp.reshape(nb, gb, 9, p).transpose(0, 2, 1, 3).reshape(nb, 9, gb * p)
    rr = jnp.concatenate([cp.real, cp.real], axis=-1)
    ii = jnp.concatenate([-cp.imag, cp.imag], axis=-1)
    sel = np.array([1, 2, 4, 8])
    return (big_t.astype(BF16), m_b.astype(BF16), m_c.astype(BF16),
            rr[:, :8], ii[:, :8], rr[:, sel], ii[:, sel])


def _s5_body(u_ref, bt_ref, mb_ref, mc_ref, prr_ref, pii_ref, srr_ref, sii_ref, y_ref, d_ref, x_ref):
    ln = SSM_CHUNK
    u = jnp.concatenate([u_ref[:, l, :].astype(BF16) for l in range(ln)], axis=1)
    d_ref[...] = jnp.dot(u, mb_ref[0], preferred_element_type=F32)
    half = SSM_GB * SSM_STATE

    def cmul(rr, ii, zz):
        return rr * zz + ii * pltpu.roll(zz, half, axis=1)

    row = lax.broadcasted_iota(jnp.int32, (8, 2 * half), 0)
    prr, pii, srr, sii = prr_ref[0], pii_ref[0], srr_ref[0], sii_ref[0]
    carry = jnp.zeros((1, 2 * half), F32)
    for tl in range(d_ref.shape[0] // 8):
        r0 = tl * 8
        pre = d_ref[r0:r0 + 8, :]
        for n, s in enumerate((1, 2, 4)):
            shifted = jnp.where(row >= s, pltpu.roll(pre, s, axis=0), 0.0)
            pre = pre + cmul(srr[n:n + 1], sii[n:n + 1], shifted)
        start = cmul(prr, pii, jnp.broadcast_to(carry, pre.shape))
        x_ref[r0:r0 + 8, :] = start + jnp.where(row >= 1, pltpu.roll(pre, 1, axis=0), 0.0)
        carry = cmul(srr[3:4], sii[3:4], carry) + pre[7:8, :]
    y = jnp.dot(u, bt_ref[0], preferred_element_type=F32)
    y = y + jnp.dot(x_ref[...].astype(BF16), mc_ref[0], preferred_element_type=F32)
    lanes = SSM_GB * SSM_GROUP
    for l in range(ln):
        y_ref[:, l, :] = y[:, l * lanes:(l + 1) * lanes]


def _s5_post_body(y_ref, u_ref, prm_ref, w_ref, o_ref):
    y = y_ref[...] + prm_ref[0:1] * u_ref[...]
    y = jax.nn.gelu(y, approximate=True)
    gate = jnp.dot(y.astype(BF16), w_ref[...], preferred_element_type=F32)
    o_ref[...] = _rms(y * jax.nn.sigmoid(gate), prm_ref[1:2]).astype(o_ref.dtype)


def s5_call(u, tables, d_skip, w_glu, norm_g, bsz, tm=512):
    t = u.shape[0]
    rows = t // SSM_CHUNK
    lanes = SSM_GB * SSM_GROUP
    blk = pl.BlockSpec((rows // bsz, SSM_CHUNK, lanes), lambda g, b: (b, 0, g))
    tab = lambda a: pl.BlockSpec((1,) + a.shape[1:], lambda g, b: (g,) + (0,) * (a.ndim - 1))
    state = pltpu.VMEM((rows // bsz, 2 * SSM_GB * SSM_STATE), F32)
    y = pl.pallas_call(
        _s5_body,
        grid=(SSM_GROUPS // SSM_GB, bsz),
        in_specs=[blk] + [tab(a) for a in tables],
        out_specs=blk,
        out_shape=jax.ShapeDtypeStruct((rows, SSM_CHUNK, D_SSM), F32),
        scratch_shapes=[state, state],
        compiler_params=_params("parallel", "parallel"),
        name="s5_scan",
    )(u.reshape(rows, SSM_CHUNK, D_SSM), *tables)
    tm = min(tm, t)
    prm = jnp.stack([d_skip.astype(F32), norm_g.astype(F32)])
    row = pl.BlockSpec((tm, D_SSM), lambda i: (i, 0))
    return pl.pallas_call(
        _s5_post_body,
        grid=(t // tm,),
        in_specs=[row, row, pl.BlockSpec((2, D_SSM), lambda i: (0, 0)),
                  pl.BlockSpec((D_SSM, D_SSM), lambda i: (0, 0))],
        out_specs=row,
        out_shape=jax.ShapeDtypeStruct((t, D_SSM), BF16),
        compiler_params=_params("parallel"),
        name="s5_post",
    )(y.reshape(t, D_SSM), u, prm, w_glu.astype(BF16))


def _att_bias(rel_bias):
    qi = np.arange(ATT_TQ)[:, None]
    kj = np.arange(ATT_WIN)[None, :]
    back = kj // CHUNK - qi // CHUNK
    valid = (back >= 0) & (back <= LEFT_CHUNKS)
    rb = rel_bias.astype(F32)
    nh = rb.shape[0]
    shift = LEFT_CHUNKS * CHUNK
    period = ATT_TQ + ATT_WIN - 1
    n_mid = ATT_WIN - (shift - MAX_REL)
    assert shift >= MAX_REL and n_mid <= 2 * MAX_REL + 1
    desc = rb[:, ::-1]
    far = lambda n: jnp.broadcast_to(desc[:, :1], (nh, n))
    g = jnp.concatenate([far(shift - MAX_REL), desc[:, :n_mid], far(ATT_TQ - 1)], axis=1)
    flat = jnp.tile(g, (1, ATT_TQ))[:, :ATT_TQ * (period - 1)]
    toeplitz = flat.reshape(nh, ATT_TQ, period - 1)[:, :, :ATT_WIN]
    return jnp.where(valid[None], toeplitz, -1e30)


def _att_body(q_ref, k_ref, v_ref, b_ref, o_ref):
    scale = ATT_HEAD_DIM ** -0.5
    lead = ATT_WIN // ATT_TQ - 1
    for i in range(q_ref.shape[1] // ATT_TQ):
        q0 = i * ATT_TQ
        k0 = max(0, q0 - (ATT_WIN - ATT_TQ))
        nk = q0 + ATT_TQ - k0
        bias = b_ref[0] if i >= lead else b_ref[0, :, ATT_WIN - nk:]
        s = lax.dot_general(q_ref[0, q0:q0 + ATT_TQ, :], k_ref[0, k0:k0 + nk, :], NT_DIMS,
                            preferred_element_type=F32) * scale + bias
        p = jnp.exp(s - jnp.max(s, axis=-1, keepdims=True))
        l = jnp.sum(p, axis=-1, keepdims=True)
        o_ref[0, q0:q0 + ATT_TQ, :] = jnp.dot(p.astype(BF16), v_ref[0, k0:k0 + nk, :],
                                              preferred_element_type=F32) / l


def attention_call(qkv, rel_bias, norm_g, bsz):
    t = qkv.shape[0]
    seq = t // bsz
    x = qkv.reshape(bsz, seq, 3 * D_ATT)
    bias = _att_bias(rel_bias)
    head = lambda c: pl.BlockSpec((1, seq, ATT_HEAD_DIM), lambda b, h: (b, 0, c * ATT_HEADS + h))
    o = pl.pallas_call(
        _att_body,
        grid=(bsz, ATT_HEADS),
        in_specs=[head(0), head(1), head(2), pl.BlockSpec((1, ATT_TQ, ATT_WIN), lambda b, h: (h, 0, 0))],
        out_specs=head(0),
        out_shape=jax.ShapeDtypeStruct((bsz, seq, D_ATT), F32),
        compiler_params=_params("parallel", "parallel"),
        name="chunk_attention",
    )(x, x, x, bias)
    return rmsnorm_call(o.reshape(t, D_ATT), norm_g, BF16)


def _top_rows(x, k):
    rows = []
    idx = lax.broadcasted_iota(jnp.int32, x.shape, 0).astype(F32)
    for i in range(k):
        m = jnp.max(x, axis=0, keepdims=True)
        rows.append(m)
        if i + 1 < k:
            first = jnp.min(jnp.where(x == m, idx, float(x.shape[0])), axis=0, keepdims=True)
            x = jnp.where(idx == first, -jnp.inf, x)
    return rows


def _peer_topk_body(q_ref, k1_ref, k2_ref, a1_ref, a2_ref, r_ref):
    half = PEER_QDIM // 2
    for h in range(PEER_HEADS):
        base = h * PEER_QDIM
        s1 = lax.dot_general(k1_ref[...], q_ref[:, base:base + half], NT_DIMS, preferred_element_type=F32)
        s2 = lax.dot_general(k2_ref[...], q_ref[:, base + half:base + PEER_QDIM], NT_DIMS,
                             preferred_element_type=F32)
        a1_ref[h] = s1
        a2_ref[h] = s2
        v1 = _top_rows(s1, PEER_TOPK)
        v2 = jnp.concatenate(_top_rows(s2, PEER_TOPK), axis=0)
        cand = jnp.concatenate([v1[a] + v2[:PEER_TOPK // (a + 1)] for a in range(PEER_TOPK)], axis=0)
        top = _top_rows(cand, PEER_TOPK)
        z = jnp.ones_like(top[0])
        for t in top[1:]:
            z = z + jnp.exp(t - top[0])
        r_ref[0, h:h + 1, :] = top[PEER_TOPK - 1]
        r_ref[1, h:h + 1, :] = v1[0]
        r_ref[2, h:h + 1, :] = v2[0:1]
        r_ref[3, h:h + 1, :] = 1.0 / z


def peer_topk_call(q, keys1, keys2, tm=256):
    t = q.shape[0]
    hk = (PEER_HEADS, PEER_KEYS, t)
    return pl.pallas_call(
        _peer_topk_body,
        grid=(t // tm,),
        in_specs=[pl.BlockSpec((tm, PEER_HEADS * PEER_QDIM), lambda i: (i, 0)),
                  pl.BlockSpec((PEER_KEYS, PEER_QDIM // 2), lambda i: (0, 0)),
                  pl.BlockSpec((PEER_KEYS, PEER_QDIM // 2), lambda i: (0, 0))],
        out_specs=[pl.BlockSpec((PEER_HEADS, PEER_KEYS, tm), lambda i: (0, 0, i)),
                   pl.BlockSpec((PEER_HEADS, PEER_KEYS, tm), lambda i: (0, 0, i)),
                   pl.BlockSpec((4, PEER_HEADS, tm), lambda i: (0, 0, i))],
        out_shape=[jax.ShapeDtypeStruct(hk, F32), jax.ShapeDtypeStruct(hk, F32),
                   jax.ShapeDtypeStruct((4, PEER_HEADS, t), F32)],
        compiler_params=_params("parallel"),
        name="peer_topk",
    )(q, keys1, keys2)


def _peer_gate_body(x_ref, u_ref, a1_ref, a2_ref, r_ref, w_ref, e2_ref, c1_ref, *, te):
    j = pl.program_id(1)

    @pl.when(j == 0)
    def _():
        for h in range(PEER_HEADS):
            e2_ref[h] = jnp.exp(a2_ref[h] - r_ref[2, h:h + 1, :])
            c1_ref[h] = jnp.exp(a1_ref[h] - r_ref[1, h:h + 1, :]) * r_ref[3, h:h + 1, :]

    ht = lax.dot_general(u_ref[...].astype(BF16), x_ref[...], NT_DIMS, preferred_element_type=F32)
    for s in range(te // PEER_KEYS):
        i1 = j * (te // PEER_KEYS) + s
        gate = None
        for h in range(PEER_HEADS):
            sums = a1_ref[h, pl.ds(i1, 1), :] + a2_ref[h]
            term = jnp.where(sums >= r_ref[0, h:h + 1, :], e2_ref[h], 0.0) * c1_ref[h, pl.ds(i1, 1), :]
            gate = term if gate is None else gate + term
        act = jax.nn.gelu(ht[s * PEER_KEYS:(s + 1) * PEER_KEYS], approximate=True)
        w_ref[:, s * PEER_KEYS:(s + 1) * PEER_KEYS] = jnp.transpose(gate * act).astype(w_ref.dtype)


def peer_gate_call(hn, u, layer, a1, a2, r, tm=1024, te=256):
    t, d = hn.shape
    e = u.shape[1]
    tm = min(tm, t)
    hk = (PEER_HEADS, PEER_KEYS, tm)
    return pl.pallas_call(
        functools.partial(_peer_gate_body, te=te),
        grid=(t // tm, e // te),
        in_specs=[pl.BlockSpec((tm, d), lambda i, j: (i, 0)),
                  pl.BlockSpec((None, te, d), lambda i, j: (layer, j, 0)),
                  pl.BlockSpec(hk, lambda i, j: (0, 0, i)),
                  pl.BlockSpec(hk, lambda i, j: (0, 0, i)),
                  pl.BlockSpec((4, PEER_HEADS, tm), lambda i, j: (0, 0, i))],
        out_specs=pl.BlockSpec((tm, te), lambda i, j: (i, j)),
        out_shape=jax.ShapeDtypeStruct((t, e), BF16),
        scratch_shapes=[pltpu.VMEM(hk, F32), pltpu.VMEM(hk, F32)],
        compiler_params=_params("parallel", "arbitrary"),
        name="peer_gate",
    )(hn, u, a1, a2, r)


def _mm_acc_body(a_ref, w_ref, o_ref):
    @pl.when(pl.program_id(2) == 0)
    def _():
        o_ref[...] = jnp.zeros_like(o_ref)

    o_ref[...] += jnp.dot(a_ref[...], w_ref[...].astype(BF16), preferred_element_type=F32)


def mm_acc_call(a, w, layer, tm=2048, tn=1024, tk=1024, name="mm_acc"):
    m, k = a.shape
    n = w.shape[2]
    tm, tn, tk = min(tm, m), min(tn, n), min(tk, k)
    return pl.pallas_call(
        _mm_acc_body,
        grid=(m // tm, n // tn, k // tk),
        in_specs=[pl.BlockSpec((tm, tk), lambda i, j, l: (i, l)),
                  pl.BlockSpec((None, tk, tn), lambda i, j, l: (layer, l, j))],
        out_specs=pl.BlockSpec((tm, tn), lambda i, j, l: (i, j)),
        out_shape=jax.ShapeDtypeStruct((m, n), F32),
        compiler_params=_params("parallel", "parallel", "arbitrary"),
        name=name,
    )(a, w)


def peer_ffn_call(h, norm_g, w_q, layer, keys1, keys2, u, v):
    hn = rmsnorm_call(h, norm_g, BF16)
    q = mm_call(hn, w_q, layer, 0, w_q.shape[-1], BF16, name="peer_q")
    a1, a2, r = peer_topk_call(q, keys1.astype(BF16), keys2.astype(BF16))
    w = peer_gate_call(hn, u, layer, a1, a2, r)
    return mm_acc_call(w, v, layer, name="peer_out")


def mixer_call(xn, h, w_in, w_out, layer, lb, hgrn_norm, s5_tables, ssm_d, ssm_w_glu, ssm_norm, rel_bias, att_norm,
               bsz):
    n_a = 4 * D_HGRN
    proj_a = mm_call(xn, w_in, layer, 0, n_a, F32, name="in_proj_a")
    proj_b = mm_call(xn, w_in, layer, n_a, D_SSM, F32, name="in_proj_b")
    proj_c = mm_call(xn, w_in, layer, n_a + D_SSM, 3 * D_ATT, BF16, name="in_proj_c")
    y_a = hgrn_call(proj_a, lb, hgrn_norm, bsz)
    y_b = s5_call(proj_b, s5_tables, ssm_d, ssm_w_glu, ssm_norm, bsz)
    y_c = attention_call(proj_c, rel_bias, att_norm, bsz)
    mixed = jnp.concatenate([y_a, y_b, y_c], axis=-1)
    return mm_call(mixed, w_out, layer, 0, w_out.shape[-1], F32, res=h, name="out_proj")


def kernel(x, w_in, w_out, hgrn_lb, hgrn_norm, ssm_a_re, ssm_a_im, ssm_log_dt, ssm_b_re, ssm_b_im, ssm_c_re, ssm_c_im, ssm_d, ssm_w_glu, ssm_norm, att_rel_bias, att_norm, norm_mix, norm_ffn, peer_w_q, peer_keys1, peer_keys2, peer_u, peer_v, norm_final):
    bsz, seq, d = x.shape
    lb_all = jnp.cumsum(jax.nn.softmax(hgrn_lb.astype(F32), axis=0), axis=0)
    lb_all = lb_all - lb_all[0]
    h = x.reshape(bsz * seq, d)
    xn = rmsnorm_call(h, norm_mix[0], BF16)
    out = None
    for layer in range(DEPTH):
        tables = _s5_prep(ssm_a_re[layer], ssm_a_im[layer], ssm_log_dt[layer], ssm_b_re[layer],
                          ssm_b_im[layer], ssm_c_re[layer], ssm_c_im[layer])
        h = mixer_call(xn, h, w_in, w_out, layer, lb_all[layer], hgrn_norm[layer], tables,
                       ssm_d[layer], ssm_w_glu[layer], ssm_norm[layer], att_rel_bias[layer],
                       att_norm[layer], bsz)
        ffn = peer_ffn_call(h, norm_ffn[layer], peer_w_q, layer, peer_keys1[layer], peer_keys2[layer],
                            peer_u, peer_v)
        if layer + 1 < DEPTH:
            h, xn = add_rmsnorm_call(h, ffn, norm_mix[layer + 1], BF16, keep_sum=True)
        else:
            out = add_rmsnorm_call(h, ffn, norm_final, F32, keep_sum=False)
    return out.reshape(bsz, seq, d)
```

```python
import functools
import math

import jax
import jax.numpy as jnp
import numpy as np
from jax import lax
from jax.experimental import pallas as pl
from jax.experimental.pallas import tpu as pltpu

F32 = jnp.float32
BF16 = jnp.bfloat16

EPS = 1e-6
DEPTH = 2
CHUNK = 64
D_HGRN = 1024
HGRN_HEADS = 8
HGRN_HEAD_DIM = D_HGRN // HGRN_HEADS
HGRN_TILE = 128
D_SSM = 1024
SSM_GROUP = 16
SSM_GROUPS = D_SSM // SSM_GROUP
SSM_STATE = 64
SSM_CHUNK = 16
SSM_GB = 8
D_ATT = 2048
ATT_HEADS = 16
ATT_HEAD_DIM = D_ATT // ATT_HEADS
LEFT_CHUNKS = 8
MAX_REL = 256
ATT_TQ = 256
ATT_WIN = ATT_TQ + LEFT_CHUNKS * CHUNK
PEER_HEADS = 8
PEER_KEYS = 128
PEER_QDIM = 256
PEER_TOPK = 16

VMEM_LIMIT = 56 * 1024 * 1024

NT_DIMS = (((1,), (1,)), ((), ()))
TN_DIMS = (((0,), (0,)), ((), ()))


def _params(*sem):
    return pltpu.CompilerParams(dimension_semantics=sem, vmem_limit_bytes=VMEM_LIMIT)


def _rms(x, g):
    ms = jnp.mean(x * x, axis=-1, keepdims=True)
    return x * lax.rsqrt(ms + EPS) * g


def _rmsnorm_body(x_ref, g_ref, o_ref):
    o_ref[...] = _rms(x_ref[...], g_ref[...]).astype(o_ref.dtype)


def _add_rmsnorm_body(a_ref, b_ref, g_ref, *out_refs):
    s = a_ref[...] + b_ref[...]
    if len(out_refs) == 2:
        out_refs[0][...] = s
    out_refs[-1][...] = _rms(s, g_ref[...]).astype(out_refs[-1].dtype)


def rmsnorm_call(x, g, out_dtype, tm=256):
    t, d = x.shape
    row = pl.BlockSpec((tm, d), lambda i: (i, 0))
    return pl.pallas_call(
        _rmsnorm_body,
        grid=(t // tm,),
        in_specs=[row, pl.BlockSpec((1, d), lambda i: (0, 0))],
        out_specs=row,
        out_shape=jax.ShapeDtypeStruct((t, d), out_dtype),
        compiler_params=_params("parallel"),
        name="rmsnorm",
    )(x, g.reshape(1, d).astype(F32))


def add_rmsnorm_call(a, b, g, out_dtype, keep_sum, tm=256):
    t, d = a.shape
    row = pl.BlockSpec((tm, d), lambda i: (i, 0))
    out_specs = [row, row] if keep_sum else row
    normed = jax.ShapeDtypeStruct((t, d), out_dtype)
    out_shape = [jax.ShapeDtypeStruct((t, d), F32), normed] if keep_sum else normed
    return pl.pallas_call(
        _add_rmsnorm_body,
        grid=(t // tm,),
        in_specs=[row, row, pl.BlockSpec((1, d), lambda i: (0, 0))],
        out_specs=out_specs,
        out_shape=out_shape,
        compiler_params=_params("parallel"),
        name="add_rmsnorm",
    )(a, b, g.reshape(1, d).astype(F32))


def _mm_body(a_ref, w_ref, o_ref):
    w = w_ref[...].astype(BF16)
    o_ref[...] = jnp.dot(a_ref[...], w, preferred_element_type=F32).astype(o_ref.dtype)


def _mm_res_body(a_ref, w_ref, r_ref, o_ref):
    acc = jnp.dot(a_ref[...], w_ref[...].astype(BF16), preferred_element_type=F32)
    o_ref[...] = (r_ref[...] + acc).astype(o_ref.dtype)


def mm_call(a, w, layer, col0, n, out_dtype, res=None, tm=1024, tn=512, name="mm"):
    m, k = a.shape
    tm, tn = min(tm, m), min(tn, n)
    c0 = col0 // tn
    assert col0 % tn == 0 and n % tn == 0
    in_specs = [pl.BlockSpec((tm, k), lambda i, j: (i, 0)),
                pl.BlockSpec((None, k, tn), lambda i, j: (layer, 0, c0 + j))]
    args = [a, w]
    body = _mm_body
    if res is not None:
        in_specs.append(pl.BlockSpec((tm, tn), lambda i, j: (i, j)))
        args.append(res)
        body = _mm_res_body
    return pl.pallas_call(
        body,
        grid=(m // tm, n // tn),
        in_specs=in_specs,
        out_specs=pl.BlockSpec((tm, tn), lambda i, j: (i, j)),
        out_shape=jax.ShapeDtypeStruct((m, n), out_dtype),
        compiler_params=_params("parallel", "arbitrary"),
        name=name,
    )(*args)


def _hgrn_consts():
    c = HGRN_TILE
    r = np.arange(c)
    t, i = r[:, None], r[None, :]
    mats_q, mats_k, masks = [], [], []
    w = c // 2
    while w >= 1:
        seg = (r // (2 * w)) * (2 * w)
        second = (r % (2 * w)) >= w
        mats_q.append(second[:, None] & (i >= (seg + w)[:, None]) & (i <= t))
        mats_k.append(~second[:, None] & (i >= t + 1) & (i <= (seg + w - 1)[:, None]))
        masks.append((seg[:, None] == seg[None, :]) & second[:, None] & ~second[None, :])
        w //= 2
    masks.append(t == i)
    mats = [i <= t, i > t] + mats_q + mats_k
    return (jnp.asarray(np.stack(mats).astype(np.float32), BF16),
            jnp.asarray(np.stack(masks).astype(np.float32), F32), len(mats_q))


def _hgrn_body(q_ref, f_ref, i_ref, g_ref, am_ref, mask_ref, prm_ref, o_ref,
               st_ref, qt_ref, kt_ref, *, levels):
    hd = HGRN_HEAD_DIM

    @pl.when(pl.program_id(1) == 0)
    def _():
        st_ref[...] = jnp.zeros_like(st_ref)

    z = f_ref[...]
    log_lb, log_1m_lb, one_m_lb, norm_g = prm_ref[0:1], prm_ref[1:2], prm_ref[2:3], prm_ref[3:4]
    e = jnp.exp(-jnp.abs(z))
    log_sig = jnp.minimum(z, 0.0) - jnp.log1p(e)
    cc = log_1m_lb + log_sig
    log_f = jnp.maximum(log_lb, cc) + jnp.log1p(jnp.exp(-jnp.abs(log_lb - cc)))
    k = one_m_lb * jnp.where(z >= 0.0, e, 1.0) / (1.0 + e)
    q = q_ref[...]
    lf_hi = log_f.astype(BF16)
    lf_lo = (log_f - lf_hi.astype(F32)).astype(BF16)

    def exponent(m):
        a = am_ref[m]
        return jnp.dot(a, lf_hi, preferred_element_type=F32) + jnp.dot(a, lf_lo, preferred_element_type=F32)

    eb = jnp.exp(exponent(0))
    decay_all = eb[HGRN_TILE - 1:HGRN_TILE, :]
    qt_ref[0] = (q * eb).astype(BF16)
    kt_ref[0] = (k * jnp.exp(exponent(1))).astype(BF16)
    for lv in range(levels):
        qt_ref[1 + lv] = (q * jnp.exp(exponent(2 + lv))).astype(BF16)
        kt_ref[1 + lv] = (k * jnp.exp(exponent(2 + levels + lv))).astype(BF16)
    qt_ref[1 + levels] = q.astype(BF16)
    kt_ref[1 + levels] = k.astype(BF16)

    for h in range(HGRN_HEADS):
        hs = slice(h * hd, (h + 1) * hd)
        scores = None
        for lv in range(levels + 1):
            sc = lax.dot_general(qt_ref[1 + lv, :, hs], kt_ref[1 + lv, :, hs], NT_DIMS,
                                 preferred_element_type=F32) * mask_ref[lv]
            scores = sc if scores is None else scores + sc
        v = i_ref[:, hs].astype(BF16)
        st = st_ref[h]
        o = jnp.dot(scores.astype(BF16), v, preferred_element_type=F32)
        o = o + lax.dot_general(qt_ref[0, :, hs], st.astype(BF16), NT_DIMS, preferred_element_type=F32)
        st_ref[h] = st * decay_all[:, hs] + lax.dot_general(v, kt_ref[0, :, hs], TN_DIMS,
                                                            preferred_element_type=F32)
        o = o * lax.rsqrt(jnp.mean(o * o, axis=-1, keepdims=True) + EPS) * norm_g[:, hs]
        o_ref[:, hs] = (o * jax.nn.silu(g_ref[:, hs])).astype(o_ref.dtype)


def hgrn_call(proj, lb, norm_g, bsz):
    t = proj.shape[0]
    nblk = t // bsz // HGRN_TILE
    am, masks, levels = _hgrn_consts()
    lb = lb.astype(F32)
    prm = jnp.stack([jnp.log(lb), jnp.log1p(-lb), 1.0 - lb, norm_g.astype(F32)])
    col = lambda c: pl.BlockSpec((HGRN_TILE, D_HGRN), lambda b, n: (b * nblk + n, c))
    whole = lambda a: pl.BlockSpec(a.shape, lambda b, n: (0,) * a.ndim)
    return pl.pallas_call(
        functools.partial(_hgrn_body, levels=levels),
        grid=(bsz, nblk),
        in_specs=[col(0), col(1), col(2), col(3), whole(am), whole(masks), whole(prm)],
        out_specs=pl.BlockSpec((HGRN_TILE, D_HGRN), lambda b, n: (b * nblk + n, 0)),
        out_shape=jax.ShapeDtypeStruct((t, D_HGRN), BF16),
        scratch_shapes=[pltpu.VMEM((HGRN_HEADS, HGRN_HEAD_DIM, HGRN_HEAD_DIM), F32),
                        pltpu.VMEM((levels + 2, HGRN_TILE, D_HGRN), BF16),
                        pltpu.VMEM((levels + 2, HGRN_TILE, D_HGRN), BF16)],
        compiler_params=_params("parallel", "arbitrary"),
        name="hgrn2",
    )(proj, proj, proj, proj, am, masks, prm)


def _s5_prep(a_re, a_im, log_dt, b_re, b_im, c_re, c_im):
    ln, gb, hp = SSM_CHUNK, SSM_GB, lax.Precision.HIGHEST
    a = lax.complex(a_re.astype(F32), a_im.astype(F32))
    adt = a * jnp.exp(log_dt.astype(F32))[:, None]
    a_bar = jnp.exp(adt)
    b_bar = ((a_bar - 1.0) / a)[..., None] * lax.complex(b_re.astype(F32), b_im.astype(F32))
    c_mat = lax.complex(c_re.astype(F32), c_im.astype(F32))
    steps = jnp.arange(ln + 1, dtype=F32)
    pw = jnp.exp(adt[:, None, :] * steps[None, :, None])
    g, p, m = b_bar.shape
    nb = g // gb
    kern = jnp.einsum('gmp,gtp,gpn->gtmn', c_mat, pw[:, :ln], b_bar, precision=hp).real
    same = jnp.asarray(np.eye(gb, dtype=bool))

    def spread(x, dst_axis):
        shape = [1] * (x.ndim + 1)
        shape[2], shape[dst_axis] = gb, gb
        return jnp.where(same.reshape(shape), jnp.expand_dims(x, dst_axis), 0).astype(BF16)

    lag_t = kern.reshape(nb, gb, ln, m, m).transpose(0, 2, 1, 4, 3)
    lag_blocks = spread(lag_t, 4).reshape(nb, ln, gb * m, gb * m)
    inc = pw[:, ln - 1::-1][:, :ln, None, :] * b_bar.transpose(0, 2, 1)[:, None, :, :]
    inc = jnp.stack([inc.real, inc.imag], axis=3)
    inc = inc.reshape(nb, gb, ln, m, 2, p).transpose(0, 2, 1, 3, 4, 5)
    m_b = spread(inc, 5).reshape(nb, ln * gb * m, 2 * gb * p)
    out = c_mat.transpose(0, 2, 1)[:, :, None, :] * pw[:, 1:ln + 1].transpose(0, 2, 1)[:, :, :, None]
    out = jnp.stack([out.real, -out.imag], axis=1)
    out = out.reshape(nb, gb, 2, p, ln, m).transpose(0, 2, 1, 3, 4, 5)
    m_c = spread(out, 5).reshape(nb, 2 * gb * p, ln * gb * m)
    cp = jnp.exp(adt[:, None, :] * (ln * jnp.arange(9, dtype=F32))[None, :, None])
    cp = cp.reshape(nb, gb, 9, p).transpose(0, 2, 1, 3).reshape(nb, 9, gb * p)
    rr = jnp.concatenate([cp.real, cp.real], axis=-1)
    ii = jnp.concatenate([-cp.imag, cp.imag], axis=-1)
    sel = np.array([1, 2, 4, 8])
    return (lag_blocks, m_b, m_c, rr[:, :8], ii[:, :8], rr[:, sel], ii[:, sel])


def _s5_body(u_ref, lag_ref, mb_ref, mc_ref, prr_ref, pii_ref, srr_ref, sii_ref, y_ref, d_ref, x_ref, bt_ref):
    ln = SSM_CHUNK
    lanes = SSM_GB * SSM_GROUP

    @pl.when(pl.program_id(1) == 0)
    def _():
        bt_ref[...] = jnp.zeros_like(bt_ref)
        for lp in range(ln):
            for l in range(lp, ln):
                bt_ref[lp * lanes:(lp + 1) * lanes, l * lanes:(l + 1) * lanes] = lag_ref[0, l - lp]

    u = jnp.concatenate([u_ref[:, l, :].astype(BF16) for l in range(ln)], axis=1)
    d_ref[...] = jnp.dot(u, mb_ref[0], preferred_element_type=F32)
    half = SSM_GB * SSM_STATE

    def cmul(rr, ii, zz):
        return rr * zz + ii * pltpu.roll(zz, half, axis=1)

    row = lax.broadcasted_iota(jnp.int32, (8, 2 * half), 0)
    prr, pii, srr, sii = prr_ref[0], pii_ref[0], srr_ref[0], sii_ref[0]
    carry = jnp.zeros((1, 2 * half), F32)
    for tl in range(d_ref.shape[0] // 8):
        r0 = tl * 8
        pre = d_ref[r0:r0 + 8, :]
        for n, s in enumerate((1, 2, 4)):
            shifted = jnp.where(row >= s, pltpu.roll(pre, s, axis=0), 0.0)
            pre = pre + cmul(srr[n:n + 1], sii[n:n + 1], shifted)
        start = cmul(prr, pii, jnp.broadcast_to(carry, pre.shape))
        x_ref[r0:r0 + 8, :] = start + jnp.where(row >= 1, pltpu.roll(pre, 1, axis=0), 0.0)
        carry = cmul(srr[3:4], sii[3:4], carry) + pre[7:8, :]
    y = jnp.dot(u, bt_ref[...], preferred_element_type=F32)
    y = y + jnp.dot(x_ref[...].astype(BF16), mc_ref[0], preferred_element_type=F32)
    for l in range(ln):
        y_ref[:, l, :] = y[:, l * lanes:(l + 1) * lanes]


def _s5_post_body(y_ref, u_ref, prm_ref, w_ref, o_ref):
    y = y_ref[...] + prm_ref[0:1] * u_ref[...]
    y = jax.nn.gelu(y, approximate=True)
    gate = jnp.dot(y.astype(BF16), w_ref[...], preferred_element_type=F32)
    o_ref[...] = _rms(y * jax.nn.sigmoid(gate), prm_ref[1:2]).astype(o_ref.dtype)


def s5_call(u, tables, d_skip, w_glu, norm_g, bsz, tm=512):
    t = u.shape[0]
    rows = t // SSM_CHUNK
    lanes = SSM_GB * SSM_GROUP
    blk = pl.BlockSpec((rows // bsz, SSM_CHUNK, lanes), lambda g, b: (b, 0, g))
    tab = lambda a: pl.BlockSpec((1,) + a.shape[1:], lambda g, b: (g,) + (0,) * (a.ndim - 1))
    state = pltpu.VMEM((rows // bsz, 2 * SSM_GB * SSM_STATE), F32)
    y = pl.pallas_call(
        _s5_body,
        grid=(SSM_GROUPS // SSM_GB, bsz),
        in_specs=[blk] + [tab(a) for a in tables],
        out_specs=blk,
        out_shape=jax.ShapeDtypeStruct((rows, SSM_CHUNK, D_SSM), F32),
        scratch_shapes=[state, state, pltpu.VMEM((SSM_CHUNK * lanes, SSM_CHUNK * lanes), BF16)],
        compiler_params=_params("parallel", "arbitrary"),
        name="s5_scan",
    )(u.reshape(rows, SSM_CHUNK, D_SSM), *tables)
    tm = min(tm, t)
    prm = jnp.stack([d_skip.astype(F32), norm_g.astype(F32)])
    row = pl.BlockSpec((tm, D_SSM), lambda i: (i, 0))
    return pl.pallas_call(
        _s5_post_body,
        grid=(t // tm,),
        in_specs=[row, row, pl.BlockSpec((2, D_SSM), lambda i: (0, 0)),
                  pl.BlockSpec((D_SSM, D_SSM), lambda i: (0, 0))],
        out_specs=row,
        out_shape=jax.ShapeDtypeStruct((t, D_SSM), BF16),
        compiler_params=_params("parallel"),
        name="s5_post",
    )(y.reshape(t, D_SSM), u, prm, w_glu.astype(BF16))


def _att_bias(rel_bias):
    qi = np.arange(ATT_TQ)[:, None]
    kj = np.arange(ATT_WIN)[None, :]
    back = kj // CHUNK - qi // CHUNK
    valid = (back >= 0) & (back <= LEFT_CHUNKS)
    rb = rel_bias.astype(F32)
    nh = rb.shape[0]
    shift = LEFT_CHUNKS * CHUNK
    period = ATT_TQ + ATT_WIN - 1
    n_mid = ATT_WIN - (shift - MAX_REL)
    assert shift >= MAX_REL and n_mid <= 2 * MAX_REL + 1
    desc = rb[:, ::-1]
    far = lambda n: jnp.broadcast_to(desc[:, :1], (nh, n))
    g = jnp.concatenate([far(shift - MAX_REL), desc[:, :n_mid], far(ATT_TQ - 1)], axis=1)
    flat = jnp.tile(g, (1, ATT_TQ))[:, :ATT_TQ * (period - 1)]
    toeplitz = flat.reshape(nh, ATT_TQ, period - 1)[:, :, :ATT_WIN]
    return jnp.where(valid[None], toeplitz, -1e30)


def _att_body(q_ref, k_ref, v_ref, b_ref, o_ref):
    scale = ATT_HEAD_DIM ** -0.5
    lead = ATT_WIN // ATT_TQ - 1
    for i in range(q_ref.shape[1] // ATT_TQ):
        q0 = i * ATT_TQ
        k0 = max(0, q0 - (ATT_WIN - ATT_TQ))
        nk = q0 + ATT_TQ - k0
        bias = b_ref[0] if i >= lead else b_ref[0, :, ATT_WIN - nk:]
        s = lax.dot_general(q_ref[0, q0:q0 + ATT_TQ, :], k_ref[0, k0:k0 + nk, :], NT_DIMS,
                            preferred_element_type=F32) * scale + bias
        p = jnp.exp(s - jnp.max(s, axis=-1, keepdims=True))
        l = jnp.sum(p, axis=-1, keepdims=True)
        o_ref[0, q0:q0 + ATT_TQ, :] = jnp.dot(p.astype(BF16), v_ref[0, k0:k0 + nk, :],
                                              preferred_element_type=F32) / l


def attention_call(qkv, rel_bias, norm_g, bsz):
    t = qkv.shape[0]
    seq = t // bsz
    x = qkv.reshape(bsz, seq, 3 * D_ATT)
    bias = _att_bias(rel_bias)
    head = lambda c: pl.BlockSpec((1, seq, ATT_HEAD_DIM), lambda b, h: (b, 0, c * ATT_HEADS + h))
    o = pl.pallas_call(
        _att_body,
        grid=(bsz, ATT_HEADS),
        in_specs=[head(0), head(1), head(2), pl.BlockSpec((1, ATT_TQ, ATT_WIN), lambda b, h: (h, 0, 0))],
        out_specs=head(0),
        out_shape=jax.ShapeDtypeStruct((bsz, seq, D_ATT), F32),
        compiler_params=_params("parallel", "parallel"),
        name="chunk_attention",
    )(x, x, x, bias)
    return rmsnorm_call(o.reshape(t, D_ATT), norm_g, BF16)


def _top_rows(x, k):
    rows = []
    idx = lax.broadcasted_iota(jnp.int32, x.shape, 0).astype(F32)
    for i in range(k):
        m = jnp.max(x, axis=0, keepdims=True)
        rows.append(m)
        if i + 1 < k:
            first = jnp.min(jnp.where(x == m, idx, float(x.shape[0])), axis=0, keepdims=True)
            x = jnp.where(idx == first, -jnp.inf, x)
    return rows


def _peer_topk_body(q_ref, k1_ref, k2_ref, a1_ref, a2_ref, r_ref):
    half = PEER_QDIM // 2
    for h in range(PEER_HEADS):
        base = h * PEER_QDIM
        s1 = lax.dot_general(k1_ref[...], q_ref[:, base:base + half], NT_DIMS, preferred_element_type=F32)
        s2 = lax.dot_general(k2_ref[...], q_ref[:, base + half:base + PEER_QDIM], NT_DIMS,
                             preferred_element_type=F32)
        a1_ref[h] = s1
        a2_ref[h] = s2
        v1 = _top_rows(s1, PEER_TOPK)
        v2 = jnp.concatenate(_top_rows(s2, PEER_TOPK), axis=0)
        cand = jnp.concatenate([v1[a] + v2[:PEER_TOPK // (a + 1)] for a in range(PEER_TOPK)], axis=0)
        top = _top_rows(cand, PEER_TOPK)
        z = jnp.ones_like(top[0])
        for t in top[1:]:
            z = z + jnp.exp(t - top[0])
        r_ref[0, h:h + 1, :] = top[PEER_TOPK - 1]
        r_ref[1, h:h + 1, :] = v1[0]
        r_ref[2, h:h + 1, :] = v2[0:1]
        r_ref[3, h:h + 1, :] = 1.0 / z


def peer_topk_call(q, keys1, keys2, tm=256):
    t = q.shape[0]
    hk = (PEER_HEADS, PEER_KEYS, t)
    return pl.pallas_call(
        _peer_topk_body,
        grid=(t // tm,),
        in_specs=[pl.BlockSpec((tm, PEER_HEADS * PEER_QDIM), lambda i: (i, 0)),
                  pl.BlockSpec((PEER_KEYS, PEER_QDIM // 2), lambda i: (0, 0)),
                  pl.BlockSpec((PEER_KEYS, PEER_QDIM // 2), lambda i: (0, 0))],
        out_specs=[pl.BlockSpec((PEER_HEADS, PEER_KEYS, tm), lambda i: (0, 0, i)),
                   pl.BlockSpec((PEER_HEADS, PEER_KEYS, tm), lambda i: (0, 0, i)),
                   pl.BlockSpec((4, PEER_HEADS, tm), lambda i: (0, 0, i))],
        out_shape=[jax.ShapeDtypeStruct(hk, F32), jax.ShapeDtypeStruct(hk, F32),
                   jax.ShapeDtypeStruct((4, PEER_HEADS, t), F32)],
        compiler_params=_params("parallel"),
        name="peer_topk",
    )(q, keys1, keys2)


def _peer_gate_body(x_ref, u_ref, a1_ref, a2_ref, r_ref, w_ref, e2_ref, c1_ref, *, te):
    j = pl.program_id(1)

    @pl.when(j == 0)
    def _():
        for h in range(PEER_HEADS):
            e2_ref[h] = jnp.exp(a2_ref[h] - r_ref[2, h:h + 1, :])
            c1_ref[h] = jnp.exp(a1_ref[h] - r_ref[1, h:h + 1, :]) * r_ref[3, h:h + 1, :]

    ht = lax.dot_general(u_ref[...].astype(BF16), x_ref[...], NT_DIMS, preferred_element_type=F32)
    for s in range(te // PEER_KEYS):
        i1 = j * (te // PEER_KEYS) + s
        gate = None
        for h in range(PEER_HEADS):
            sums = a1_ref[h, pl.ds(i1, 1), :] + a2_ref[h]
            term = jnp.where(sums >= r_ref[0, h:h + 1, :], e2_ref[h], 0.0) * c1_ref[h, pl.ds(i1, 1), :]
            gate = term if gate is None else gate + term
        act = jax.nn.gelu(ht[s * PEER_KEYS:(s + 1) * PEER_KEYS], approximate=True)
        w_ref[:, s * PEER_KEYS:(s + 1) * PEER_KEYS] = jnp.transpose(gate * act).astype(w_ref.dtype)


def peer_gate_call(hn, u, layer, a1, a2, r, tm=1024, te=256):
    t, d = hn.shape
    e = u.shape[1]
    tm = min(tm, t)
    hk = (PEER_HEADS, PEER_KEYS, tm)
    return pl.pallas_call(
        functools.partial(_peer_gate_body, te=te),
        grid=(t // tm, e // te),
        in_specs=[pl.BlockSpec((tm, d), lambda i, j: (i, 0)),
                  pl.BlockSpec((None, te, d), lambda i, j: (layer, j, 0)),
                  pl.BlockSpec(hk, lambda i, j: (0, 0, i)),
                  pl.BlockSpec(hk, lambda i, j: (0, 0, i)),
                  pl.BlockSpec((4, PEER_HEADS, tm), lambda i, j: (0, 0, i))],
        out_specs=pl.BlockSpec((tm, te), lambda i, j: (i, j)),
        out_shape=jax.ShapeDtypeStruct((t, e), BF16),
        scratch_shapes=[pltpu.VMEM(hk, F32), pltpu.VMEM(hk, F32)],
        compiler_params=_params("parallel", "arbitrary"),
        name="peer_gate",
    )(hn, u, a1, a2, r)


def _mm_acc_body(a_ref, w_ref, o_ref):
    @pl.when(pl.program_id(2) == 0)
    def _():
        o_ref[...] = jnp.zeros_like(o_ref)

    o_ref[...] += jnp.dot(a_ref[...], w_ref[...].astype(BF16), preferred_element_type=F32)


def mm_acc_call(a, w, layer, tm=2048, tn=1024, tk=1024, name="mm_acc"):
    m, k = a.shape
    n = w.shape[2]
    tm, tn, tk = min(tm, m), min(tn, n), min(tk, k)
    return pl.pallas_call(
        _mm_acc_body,
        grid=(m // tm, n // tn, k // tk),
        in_specs=[pl.BlockSpec((tm, tk), lambda i, j, l: (i, l)),
                  pl.BlockSpec((None, tk, tn), lambda i, j, l: (layer, l, j))],
        out_specs=pl.BlockSpec((tm, tn), lambda i, j, l: (i, j)),
        out_shape=jax.ShapeDtypeStruct((m, n), F32),
        compiler_params=_params("parallel", "parallel", "arbitrary"),
        name=name,
    )(a, w)


def peer_ffn_call(h, norm_g, w_q, layer, keys1, keys2, u, v):
    hn = rmsnorm_call(h, norm_g, BF16)
    q = mm_call(hn, w_q, layer, 0, w_q.shape[-1], BF16, name="peer_q")
    a1, a2, r = peer_topk_call(q, keys1.astype(BF16), keys2.astype(BF16))
    w = peer_gate_call(hn, u, layer, a1, a2, r)
    return mm_acc_call(w, v, layer, name="peer_out")


def mixer_call(xn, h, w_in, w_out, layer, lb, hgrn_norm, s5_tables, ssm_d, ssm_w_glu, ssm_norm, rel_bias, att_norm,
               bsz):
    n_a = 4 * D_HGRN
    proj_a = mm_call(xn, w_in, layer, 0, n_a, F32, name="in_proj_a")
    proj_b = mm_call(xn, w_in, layer, n_a, D_SSM, F32, name="in_proj_b")
    proj_c = mm_call(xn, w_in, layer, n_a + D_SSM, 3 * D_ATT, BF16, name="in_proj_c")
    y_a = hgrn_call(proj_a, lb, hgrn_norm, bsz)
    y_b = s5_call(proj_b, s5_tables, ssm_d, ssm_w_glu, ssm_norm, bsz)
    y_c = attention_call(proj_c, rel_bias, att_norm, bsz)
    mixed = jnp.concatenate([y_a, y_b, y_c], axis=-1)
    return mm_call(mixed, w_out, layer, 0, w_out.shape[-1], F32, res=h, name="out_proj")


def kernel(x, w_in, w_out, hgrn_lb, hgrn_norm, ssm_a_re, ssm_a_im, ssm_log_dt, ssm_b_re, ssm_b_im, ssm_c_re, ssm_c_im, ssm_d, ssm_w_glu, ssm_norm, att_rel_bias, att_norm, norm_mix, norm_ffn, peer_w_q, peer_keys1, peer_keys2, peer_u, peer_v, norm_final):
    bsz, seq, d = x.shape
    lb_all = jnp.cumsum(jax.nn.softmax(hgrn_lb.astype(F32), axis=0), axis=0)
    lb_all = lb_all - lb_all[0]
    h = x.reshape(bsz * seq, d)
    xn = rmsnorm_call(h, norm_mix[0], BF16)
    out = None
    for layer in range(DEPTH):
        tables = _s5_prep(ssm_a_re[layer], ssm_a_im[layer], ssm_log_dt[layer], ssm_b_re[layer],
                          ssm_b_im[layer], ssm_c_re[layer], ssm_c_im[layer])
        h = mixer_call(xn, h, w_in, w_out, layer, lb_all[layer], hgrn_norm[layer], tables,
                       ssm_d[layer], ssm_w_glu[layer], ssm_norm[layer], att_rel_bias[layer],
                       att_norm[layer], bsz)
        ffn = peer_ffn_call(h, norm_ffn[layer], peer_w_q, layer, peer_keys1[layer], peer_keys2[layer],
                            peer_u, peer_v)
        if layer + 1 < DEPTH:
            h, xn = add_rmsnorm_call(h, ffn, norm_mix[layer + 1], BF16, keep_sum=True)
        else:
            out = add_rmsnorm_call(h, ffn, norm_final, F32, keep_sum=False)
    return out.reshape(bsz, seq, d)
```

```python
import functools
import math

import jax
import jax.numpy as jnp
import numpy as np
from jax import lax
from jax.experimental import pallas as pl
from jax.experimental.pallas import tpu as pltpu

F32 = jnp.float32
BF16 = jnp.bfloat16

EPS = 1e-6
DEPTH = 2
CHUNK = 64
D_HGRN = 1024
HGRN_HEADS = 8
HGRN_HEAD_DIM = D_HGRN // HGRN_HEADS
HGRN_TILE = 128
D_SSM = 1024
SSM_GROUP = 16
SSM_GROUPS = D_SSM // SSM_GROUP
SSM_STATE = 64
SSM_CHUNK = 16
SSM_GB = 8
D_ATT = 2048
ATT_HEADS = 16
ATT_HEAD_DIM = D_ATT // ATT_HEADS
LEFT_CHUNKS = 8
MAX_REL = 256
ATT_TQ = 256
ATT_WIN = ATT_TQ + LEFT_CHUNKS * CHUNK
PEER_HEADS = 8
PEER_KEYS = 128
PEER_QDIM = 256
PEER_TOPK = 16

VMEM_LIMIT = 56 * 1024 * 1024

NT_DIMS = (((1,), (1,)), ((), ()))
TN_DIMS = (((0,), (0,)), ((), ()))


def _params(*sem):
    return pltpu.CompilerParams(dimension_semantics=sem, vmem_limit_bytes=VMEM_LIMIT)


def _rms(x, g):
    ms = jnp.mean(x * x, axis=-1, keepdims=True)
    return x * lax.rsqrt(ms + EPS) * g


def _rmsnorm_body(x_ref, g_ref, o_ref):
    o_ref[...] = _rms(x_ref[...], g_ref[...]).astype(o_ref.dtype)


def _add_rmsnorm_body(a_ref, b_ref, g_ref, *out_refs):
    s = a_ref[...] + b_ref[...]
    if len(out_refs) == 2:
        out_refs[0][...] = s
    out_refs[-1][...] = _rms(s, g_ref[...]).astype(out_refs[-1].dtype)


def rmsnorm_call(x, g, out_dtype, tm=256):
    t, d = x.shape
    row = pl.BlockSpec((tm, d), lambda i: (i, 0))
    return pl.pallas_call(
        _rmsnorm_body,
        grid=(t // tm,),
        in_specs=[row, pl.BlockSpec((1, d), lambda i: (0, 0))],
        out_specs=row,
        out_shape=jax.ShapeDtypeStruct((t, d), out_dtype),
        compiler_params=_params("parallel"),
        name="rmsnorm",
    )(x, g.reshape(1, d).astype(F32))


def add_rmsnorm_call(a, b, g, out_dtype, keep_sum, tm=256):
    t, d = a.shape
    row = pl.BlockSpec((tm, d), lambda i: (i, 0))
    out_specs = [row, row] if keep_sum else row
    normed = jax.ShapeDtypeStruct((t, d), out_dtype)
    out_shape = [jax.ShapeDtypeStruct((t, d), F32), normed] if keep_sum else normed
    return pl.pallas_call(
        _add_rmsnorm_body,
        grid=(t // tm,),
        in_specs=[row, row, pl.BlockSpec((1, d), lambda i: (0, 0))],
        out_specs=out_specs,
        out_shape=out_shape,
        compiler_params=_params("parallel"),
        name="add_rmsnorm",
    )(a, b, g.reshape(1, d).astype(F32))


def _mm_body(a_ref, w_ref, o_ref):
    w = w_ref[...].astype(BF16)
    o_ref[...] = jnp.dot(a_ref[...], w, preferred_element_type=F32).astype(o_ref.dtype)


def _mm_res_body(a_ref, w_ref, r_ref, o_ref):
    acc = jnp.dot(a_ref[...], w_ref[...].astype(BF16), preferred_element_type=F32)
    o_ref[...] = (r_ref[...] + acc).astype(o_ref.dtype)


def mm_call(a, w, layer, col0, n, out_dtype, res=None, tm=1024, tn=512, name="mm"):
    m, k = a.shape
    tm, tn = min(tm, m), min(tn, n)
    c0 = col0 // tn
    assert col0 % tn == 0 and n % tn == 0
    in_specs = [pl.BlockSpec((tm, k), lambda i, j: (i, 0)),
                pl.BlockSpec((None, k, tn), lambda i, j: (layer, 0, c0 + j))]
    args = [a, w]
    body = _mm_body
    if res is not None:
        in_specs.append(pl.BlockSpec((tm, tn), lambda i, j: (i, j)))
        args.append(res)
        body = _mm_res_body
    return pl.pallas_call(
        body,
        grid=(m // tm, n // tn),
        in_specs=in_specs,
        out_specs=pl.BlockSpec((tm, tn), lambda i, j: (i, j)),
        out_shape=jax.ShapeDtypeStruct((m, n), out_dtype),
        compiler_params=_params("parallel", "arbitrary"),
        name=name,
    )(*args)


def _hgrn_consts():
    c = HGRN_TILE
    r = np.arange(c)
    t, i = r[:, None], r[None, :]
    mats_q, mats_k, masks = [], [], []
    w = c // 2
    while w >= 1:
        seg = (r // (2 * w)) * (2 * w)
        second = (r % (2 * w)) >= w
        mats_q.append(second[:, None] & (i >= (seg + w)[:, None]) & (i <= t))
        mats_k.append(~second[:, None] & (i >= t + 1) & (i <= (seg + w - 1)[:, None]))
        masks.append((seg[:, None] == seg[None, :]) & second[:, None] & ~second[None, :])
        w //= 2
    masks.append(t == i)
    mats = [i <= t, i > t] + mats_q + mats_k
    return (jnp.asarray(np.stack(mats).astype(np.float32), BF16),
            jnp.asarray(np.stack(masks).astype(np.float32), F32), len(mats_q))


def _hgrn_body(q_ref, f_ref, i_ref, g_ref, am_ref, mask_ref, prm_ref, o_ref,
               st_ref, qt_ref, kt_ref, *, levels):
    hd = HGRN_HEAD_DIM

    @pl.when(pl.program_id(1) == 0)
    def _():
        st_ref[...] = jnp.zeros_like(st_ref)

    z = f_ref[...]
    log_lb, log_1m_lb, one_m_lb, norm_g = prm_ref[0:1], prm_ref[1:2], prm_ref[2:3], prm_ref[3:4]
    e = jnp.exp(-jnp.abs(z))
    log_sig = jnp.minimum(z, 0.0) - jnp.log1p(e)
    cc = log_1m_lb + log_sig
    log_f = jnp.maximum(log_lb, cc) + jnp.log1p(jnp.exp(-jnp.abs(log_lb - cc)))
    k = one_m_lb * jnp.where(z >= 0.0, e, 1.0) / (1.0 + e)
    q = q_ref[...]
    lf_hi = log_f.astype(BF16)
    lf_lo = (log_f - lf_hi.astype(F32)).astype(BF16)

    def exponent(m):
        a = am_ref[m]
        return jnp.dot(a, lf_hi, preferred_element_type=F32) + jnp.dot(a, lf_lo, preferred_element_type=F32)

    eb = jnp.exp(exponent(0))
    decay_all = eb[HGRN_TILE - 1:HGRN_TILE, :]
    qt_ref[0] = (q * eb).astype(BF16)
    kt_ref[0] = (k * jnp.exp(exponent(1))).astype(BF16)
    for lv in range(levels):
        qt_ref[1 + lv] = (q * jnp.exp(exponent(2 + lv))).astype(BF16)
        kt_ref[1 + lv] = (k * jnp.exp(exponent(2 + levels + lv))).astype(BF16)
    qt_ref[1 + levels] = q.astype(BF16)
    kt_ref[1 + levels] = k.astype(BF16)

    for h in range(HGRN_HEADS):
        hs = slice(h * hd, (h + 1) * hd)
        scores = None
        for lv in range(levels + 1):
            sc = lax.dot_general(qt_ref[1 + lv, :, hs], kt_ref[1 + lv, :, hs], NT_DIMS,
                                 preferred_element_type=F32) * mask_ref[lv]
            scores = sc if scores is None else scores + sc
        v = i_ref[:, hs].astype(BF16)
        st = st_ref[h]
        o = jnp.dot(scores.astype(BF16), v, preferred_element_type=F32)
        o = o + lax.dot_general(qt_ref[0, :, hs], st.astype(BF16), NT_DIMS, preferred_element_type=F32)
        st_ref[h] = st * decay_all[:, hs] + lax.dot_general(v, kt_ref[0, :, hs], TN_DIMS,
                                                            preferred_element_type=F32)
        o = o * lax.rsqrt(jnp.mean(o * o, axis=-1, keepdims=True) + EPS) * norm_g[:, hs]
        o_ref[:, hs] = (o * jax.nn.silu(g_ref[:, hs])).astype(o_ref.dtype)


def hgrn_call(proj, lb, norm_g, bsz):
    t = proj.shape[0]
    nblk = t // bsz // HGRN_TILE
    am, masks, levels = _hgrn_consts()
    lb = lb.astype(F32)
    prm = jnp.stack([jnp.log(lb), jnp.log1p(-lb), 1.0 - lb, norm_g.astype(F32)])
    col = lambda c: pl.BlockSpec((HGRN_TILE, D_HGRN), lambda b, n: (b * nblk + n, c))
    whole = lambda a: pl.BlockSpec(a.shape, lambda b, n: (0,) * a.ndim)
    return pl.pallas_call(
        functools.partial(_hgrn_body, levels=levels),
        grid=(bsz, nblk),
        in_specs=[col(0), col(1), col(2), col(3), whole(am), whole(masks), whole(prm)],
        out_specs=pl.BlockSpec((HGRN_TILE, D_HGRN), lambda b, n: (b * nblk + n, 0)),
        out_shape=jax.ShapeDtypeStruct((t, D_HGRN), BF16),
        scratch_shapes=[pltpu.VMEM((HGRN_HEADS, HGRN_HEAD_DIM, HGRN_HEAD_DIM), F32),
                        pltpu.VMEM((levels + 2, HGRN_TILE, D_HGRN), BF16),
                        pltpu.VMEM((levels + 2, HGRN_TILE, D_HGRN), BF16)],
        compiler_params=_params("parallel", "arbitrary"),
        name="hgrn2",
    )(proj, proj, proj, proj, am, masks, prm)


def _s5_prep(a_re, a_im, log_dt, b_re, b_im, c_re, c_im):
    ln, gb, hp = SSM_CHUNK, SSM_GB, lax.Precision.HIGHEST
    a = lax.complex(a_re.astype(F32), a_im.astype(F32))
    adt = a * jnp.exp(log_dt.astype(F32))[:, None]
    a_bar = jnp.exp(adt)
    b_bar = ((a_bar - 1.0) / a)[..., None] * lax.complex(b_re.astype(F32), b_im.astype(F32))
    c_mat = lax.complex(c_re.astype(F32), c_im.astype(F32))
    steps = jnp.arange(ln + 1, dtype=F32)
    pw = jnp.exp(adt[:, None, :] * steps[None, :, None])
    g, p, m = b_bar.shape
    nb = g // gb
    kern = jnp.einsum('gmp,gtp,gpn->gtmn', c_mat, pw[:, :ln], b_bar, precision=hp).real
    same = jnp.asarray(np.eye(gb, dtype=bool))

    def spread(x, dst_axis):
        shape = [1] * (x.ndim + 1)
        shape[2], shape[dst_axis] = gb, gb
        return jnp.where(same.reshape(shape), jnp.expand_dims(x, dst_axis), 0).astype(BF16)

    lag_t = kern.reshape(nb, gb, ln, m, m).transpose(0, 2, 1, 4, 3)
    lag_blocks = spread(lag_t, 4).reshape(nb, ln, gb * m, gb * m)
    inc = pw[:, ln - 1::-1][:, :ln, None, :] * b_bar.transpose(0, 2, 1)[:, None, :, :]
    inc = jnp.stack([inc.real, inc.imag], axis=3)
    inc = inc.reshape(nb, gb, ln, m, 2, p).transpose(0, 2, 1, 3, 4, 5)
    m_b = spread(inc, 5).reshape(nb, ln * gb * m, 2 * gb * p)
    out = c_mat.transpose(0, 2, 1)[:, :, None, :] * pw[:, 1:ln + 1].transpose(0, 2, 1)[:, :, :, None]
    out = jnp.stack([out.real, -out.imag], axis=1)
    out = out.reshape(nb, gb, 2, p, ln, m).transpose(0, 2, 1, 3, 4, 5)
    m_c = spread(out, 5).reshape(nb, 2 * gb * p, ln * gb * m)
    cp = jnp.exp(adt[:, None, :] * (ln * jnp.arange(9, dtype=F32))[None, :, None])
    cp = cp.reshape(nb, gb, 9, p).transpose(0, 2, 1, 3).reshape(nb, 9, gb * p)
    rr = jnp.concatenate([cp.real, cp.real], axis=-1)
    ii = jnp.concatenate([-cp.imag, cp.imag], axis=-1)
    sel = np.array([1, 2, 4, 8])
    return (lag_blocks, m_b, m_c, rr[:, :8], ii[:, :8], rr[:, sel], ii[:, sel])


def _s5_body(u_ref, lag_ref, mb_ref, mc_ref, prr_ref, pii_ref, srr_ref, sii_ref, y_ref, d_ref, x_ref, bt_ref):
    ln = SSM_CHUNK
    lanes = SSM_GB * SSM_GROUP

    @pl.when(pl.program_id(1) == 0)
    def _():
        bt_ref[...] = jnp.zeros_like(bt_ref)
        for lp in range(ln):
            for l in range(lp, ln):
                bt_ref[lp * lanes:(lp + 1) * lanes, l * lanes:(l + 1) * lanes] = lag_ref[0, l - lp]

    u = jnp.concatenate([u_ref[:, l, :].astype(BF16) for l in range(ln)], axis=1)
    d_ref[...] = jnp.dot(u, mb_ref[0], preferred_element_type=F32)
    half = SSM_GB * SSM_STATE

    def cmul(rr, ii, zz):
        return rr * zz + ii * pltpu.roll(zz, half, axis=1)

    row = lax.broadcasted_iota(jnp.int32, (8, 2 * half), 0)
    prr, pii, srr, sii = prr_ref[0], pii_ref[0], srr_ref[0], sii_ref[0]
    carry = jnp.zeros((1, 2 * half), F32)
    for tl in range(d_ref.shape[0] // 8):
        r0 = tl * 8
        pre = d_ref[r0:r0 + 8, :]
        for n, s in enumerate((1, 2, 4)):
            shifted = jnp.where(row >= s, pltpu.roll(pre, s, axis=0), 0.0)
            pre = pre + cmul(srr[n:n + 1], sii[n:n + 1], shifted)
        start = cmul(prr, pii, jnp.broadcast_to(carry, pre.shape))
        x_ref[r0:r0 + 8, :] = start + jnp.where(row >= 1, pltpu.roll(pre, 1, axis=0), 0.0)
        carry = cmul(srr[3:4], sii[3:4], carry) + pre[7:8, :]
    y = jnp.dot(u, bt_ref[...], preferred_element_type=F32)
    y = y + jnp.dot(x_ref[...].astype(BF16), mc_ref[0], preferred_element_type=F32)
    for l in range(ln):
        y_ref[:, l, :] = y[:, l * lanes:(l + 1) * lanes]


def _s5_post_body(y_ref, u_ref, prm_ref, w_ref, o_ref):
    y = y_ref[...] + prm_ref[0:1] * u_ref[...]
    y = jax.nn.gelu(y, approximate=True)
    gate = jnp.dot(y.astype(BF16), w_ref[...], preferred_element_type=F32)
    o_ref[...] = _rms(y * jax.nn.sigmoid(gate), prm_ref[1:2]).astype(o_ref.dtype)


def s5_call(u, tables, d_skip, w_glu, norm_g, bsz, tm=512):
    t = u.shape[0]
    rows = t // SSM_CHUNK
    lanes = SSM_GB * SSM_GROUP
    blk = pl.BlockSpec((rows // bsz, SSM_CHUNK, lanes), lambda g, b: (b, 0, g))
    tab = lambda a: pl.BlockSpec((1,) + a.shape[1:], lambda g, b: (g,) + (0,) * (a.ndim - 1))
    state = pltpu.VMEM((rows // bsz, 2 * SSM_GB * SSM_STATE), F32)
    y = pl.pallas_call(
        _s5_body,
        grid=(SSM_GROUPS // SSM_GB, bsz),
        in_specs=[blk] + [tab(a) for a in tables],
        out_specs=blk,
        out_shape=jax.ShapeDtypeStruct((rows, SSM_CHUNK, D_SSM), F32),
        scratch_shapes=[state, state, pltpu.VMEM((SSM_CHUNK * lanes, SSM_CHUNK * lanes), BF16)],
        compiler_params=_params("parallel", "arbitrary"),
        name="s5_scan",
    )(u.reshape(rows, SSM_CHUNK, D_SSM), *tables)
    tm = min(tm, t)
    prm = jnp.stack([d_skip.astype(F32), norm_g.astype(F32)])
    row = pl.BlockSpec((tm, D_SSM), lambda i: (i, 0))
    return pl.pallas_call(
        _s5_post_body,
        grid=(t // tm,),
        in_specs=[row, row, pl.BlockSpec((2, D_SSM), lambda i: (0, 0)),
                  pl.BlockSpec((D_SSM, D_SSM), lambda i: (0, 0))],
        out_specs=row,
        out_shape=jax.ShapeDtypeStruct((t, D_SSM), BF16),
        compiler_params=_params("parallel"),
        name="s5_post",
    )(y.reshape(t, D_SSM), u, prm, w_glu.astype(BF16))


def _att_bias(rel_bias):
    qi = np.arange(ATT_TQ)[:, None]
    kj = np.arange(ATT_WIN)[None, :]
    back = kj // CHUNK - qi // CHUNK
    valid = (back >= 0) & (back <= LEFT_CHUNKS)
    rb = rel_bias.astype(F32)
    nh = rb.shape[0]
    shift = LEFT_CHUNKS * CHUNK
    period = ATT_TQ + ATT_WIN - 1
    n_mid = ATT_WIN - (shift - MAX_REL)
    assert shift >= MAX_REL and n_mid <= 2 * MAX_REL + 1
    desc = rb[:, ::-1]
    far = lambda n: jnp.broadcast_to(desc[:, :1], (nh, n))
    g = jnp.concatenate([far(shift - MAX_REL), desc[:, :n_mid], far(ATT_TQ - 1)], axis=1)
    flat = jnp.tile(g, (1, ATT_TQ))[:, :ATT_TQ * (period - 1)]
    toeplitz = flat.reshape(nh, ATT_TQ, period - 1)[:, :, :ATT_WIN]
    return jnp.where(valid[None], toeplitz, -1e30)


def _att_body(q_ref, k_ref, v_ref, b_ref, o_ref):
    scale = ATT_HEAD_DIM ** -0.5
    lead = ATT_WIN // ATT_TQ - 1
    for i in range(q_ref.shape[1] // ATT_TQ):
        q0 = i * ATT_TQ
        k0 = max(0, q0 - (ATT_WIN - ATT_TQ))
        nk = q0 + ATT_TQ - k0
        bias = b_ref[0] if i >= lead else b_ref[0, :, ATT_WIN - nk:]
        s = lax.dot_general(q_ref[0, q0:q0 + ATT_TQ, :], k_ref[0, k0:k0 + nk, :], NT_DIMS,
                            preferred_element_type=F32) * scale + bias
        p = jnp.exp(s - jnp.max(s, axis=-1, keepdims=True))
        l = jnp.sum(p, axis=-1, keepdims=True)
        o_ref[0, q0:q0 + ATT_TQ, :] = jnp.dot(p.astype(BF16), v_ref[0, k0:k0 + nk, :],
                                              preferred_element_type=F32) / l


def attention_call(qkv, rel_bias, norm_g, bsz):
    t = qkv.shape[0]
    seq = t // bsz
    x = qkv.reshape(bsz, seq, 3 * D_ATT)
    bias = _att_bias(rel_bias)
    head = lambda c: pl.BlockSpec((1, seq, ATT_HEAD_DIM), lambda b, h: (b, 0, c * ATT_HEADS + h))
    o = pl.pallas_call(
        _att_body,
        grid=(bsz, ATT_HEADS),
        in_specs=[head(0), head(1), head(2), pl.BlockSpec((1, ATT_TQ, ATT_WIN), lambda b, h: (h, 0, 0))],
        out_specs=head(0),
        out_shape=jax.ShapeDtypeStruct((bsz, seq, D_ATT), F32),
        compiler_params=_params("parallel", "parallel"),
        name="chunk_attention",
    )(x, x, x, bias)
    return rmsnorm_call(o.reshape(t, D_ATT), norm_g, BF16)


def _top_rows(x, k):
    vals, where_ = [], []
    idx = lax.broadcasted_iota(jnp.int32, x.shape, 0).astype(F32)
    for _ in range(k):
        m = jnp.max(x, axis=0, keepdims=True)
        first = jnp.min(jnp.where(x == m, idx, float(x.shape[0])), axis=0, keepdims=True)
        x = jnp.where(idx == first, -jnp.inf, x)
        vals.append(m)
        where_.append(first)
    return vals, where_


def _peer_topk_body(q_ref, k1_ref, k2_ref, n1_ref, c1_ref, rk_ref, e2_ref):
    half = PEER_QDIM // 2
    k = PEER_TOPK
    for h in range(PEER_HEADS):
        base = h * PEER_QDIM
        s1 = lax.dot_general(k1_ref[...], q_ref[:, base:base + half], NT_DIMS, preferred_element_type=F32)
        s2 = lax.dot_general(k2_ref[...], q_ref[:, base + half:base + PEER_QDIM], NT_DIMS,
                             preferred_element_type=F32)
        v1, at1 = _top_rows(s1, k)
        v2, at2 = _top_rows(s2, k)
        v2 = jnp.concatenate(v2, axis=0)
        cand = jnp.concatenate([v1[a] + v2[:k // (a + 1)] for a in range(k)], axis=0)
        top, _ = _top_rows(cand, k)
        tau = top[k - 1]
        z = jnp.ones_like(tau)
        for t in top[1:]:
            z = z + jnp.exp(t - top[0])
        idx = lax.broadcasted_iota(jnp.int32, s1.shape, 0).astype(F32)
        count = jnp.zeros_like(s1)
        rank = jnp.full_like(s2, float(k))
        for a in range(k):
            n_a = jnp.sum(jnp.where(v1[a] + v2 >= tau, 1.0, 0.0), axis=0, keepdims=True)
            count = jnp.where(idx == at1[a], n_a, count)
            rank = jnp.where(idx == at2[a], float(a), rank)
        n1_ref[h] = count
        c1_ref[h] = jnp.exp(s1 - v1[0]) / z
        rk_ref[h] = rank.astype(rk_ref.dtype)
        e2_ref[h] = jnp.exp(s2 - v2[0:1]).astype(e2_ref.dtype)


def peer_topk_call(q, keys1, keys2, tm=256):
    t = q.shape[0]
    hk = (PEER_HEADS, PEER_KEYS, t)
    blk = pl.BlockSpec((PEER_HEADS, PEER_KEYS, tm), lambda i: (0, 0, i))
    return pl.pallas_call(
        _peer_topk_body,
        grid=(t // tm,),
        in_specs=[pl.BlockSpec((tm, PEER_HEADS * PEER_QDIM), lambda i: (i, 0)),
                  pl.BlockSpec((PEER_KEYS, PEER_QDIM // 2), lambda i: (0, 0)),
                  pl.BlockSpec((PEER_KEYS, PEER_QDIM // 2), lambda i: (0, 0))],
        out_specs=[blk, blk, blk, blk],
        out_shape=[jax.ShapeDtypeStruct(hk, F32), jax.ShapeDtypeStruct(hk, F32),
                   jax.ShapeDtypeStruct(hk, BF16), jax.ShapeDtypeStruct(hk, BF16)],
        compiler_params=_params("parallel"),
        name="peer_topk",
    )(q, keys1, keys2)


def _peer_gate_body(x_ref, u_ref, n1_ref, c1_ref, rk_ref, e2_ref, w_ref, *, te):
    j = pl.program_id(1)
    ht = lax.dot_general(u_ref[...].astype(BF16), x_ref[...], NT_DIMS, preferred_element_type=F32)
    for s in range(te // PEER_KEYS):
        i1 = j * (te // PEER_KEYS) + s
        gate = None
        for h in range(PEER_HEADS):
            count = n1_ref[h, pl.ds(i1, 1), :].astype(BF16)
            factor = c1_ref[h, pl.ds(i1, 1), :].astype(BF16)
            term = jnp.where(rk_ref[h] < count, e2_ref[h], jnp.zeros((), BF16)) * factor
            gate = term if gate is None else gate + term
        act = jax.nn.gelu(ht[s * PEER_KEYS:(s + 1) * PEER_KEYS], approximate=True).astype(BF16)
        w_ref[:, s * PEER_KEYS:(s + 1) * PEER_KEYS] = jnp.transpose(gate * act)


def peer_gate_call(hn, u, layer, n1, c1, rk, e2, tm=1024, te=256):
    t, d = hn.shape
    e = u.shape[1]
    tm = min(tm, t)
    hk = pl.BlockSpec((PEER_HEADS, PEER_KEYS, tm), lambda i, j: (0, 0, i))
    return pl.pallas_call(
        functools.partial(_peer_gate_body, te=te),
        grid=(t // tm, e // te),
        in_specs=[pl.BlockSpec((tm, d), lambda i, j: (i, 0)),
                  pl.BlockSpec((None, te, d), lambda i, j: (layer, j, 0)),
                  hk, hk, hk, hk],
        out_specs=pl.BlockSpec((tm, te), lambda i, j: (i, j)),
        out_shape=jax.ShapeDtypeStruct((t, e), BF16),
        compiler_params=_params("parallel", "arbitrary"),
        name="peer_gate",
    )(hn, u, n1, c1, rk, e2)


def _mm_acc_body(a_ref, w_ref, o_ref):
    @pl.when(pl.program_id(2) == 0)
    def _():
        o_ref[...] = jnp.zeros_like(o_ref)

    o_ref[...] += jnp.dot(a_ref[...], w_ref[...].astype(BF16), preferred_element_type=F32)


def mm_acc_call(a, w, layer, tm=2048, tn=1024, tk=1024, name="mm_acc"):
    m, k = a.shape
    n = w.shape[2]
    tm, tn, tk = min(tm, m), min(tn, n), min(tk, k)
    return pl.pallas_call(
        _mm_acc_body,
        grid=(m // tm, n // tn, k // tk),
        in_specs=[pl.BlockSpec((tm, tk), lambda i, j, l: (i, l)),
                  pl.BlockSpec((None, tk, tn), lambda i, j, l: (layer, l, j))],
        out_specs=pl.BlockSpec((tm, tn), lambda i, j, l: (i, j)),
        out_shape=jax.ShapeDtypeStruct((m, n), F32),
        compiler_params=_params("parallel", "parallel", "arbitrary"),
        name=name,
    )(a, w)


def peer_ffn_call(h, norm_g, w_q, layer, keys1, keys2, u, v):
    hn = rmsnorm_call(h, norm_g, BF16)
    q = mm_call(hn, w_q, layer, 0, w_q.shape[-1], BF16, name="peer_q")
    n1, c1, rk, e2 = peer_topk_call(q, keys1.astype(BF16), keys2.astype(BF16))
    w = peer_gate_call(hn, u, layer, n1, c1, rk, e2)
    return mm_acc_call(w, v, layer, name="peer_out")


def mixer_call(xn, h, w_in, w_out, layer, lb, hgrn_norm, s5_tables, ssm_d, ssm_w_glu, ssm_norm, rel_bias, att_norm,
               bsz):
    n_a = 4 * D_HGRN
    proj_a = mm_call(xn, w_in, layer, 0, n_a, F32, name="in_proj_a")
    proj_b = mm_call(xn, w_in, layer, n_a, D_SSM, F32, name="in_proj_b")
    proj_c = mm_call(xn, w_in, layer, n_a + D_SSM, 3 * D_ATT, BF16, name="in_proj_c")
    y_a = hgrn_call(proj_a, lb, hgrn_norm, bsz)
    y_b = s5_call(proj_b, s5_tables, ssm_d, ssm_w_glu, ssm_norm, bsz)
    y_c = attention_call(proj_c, rel_bias, att_norm, bsz)
    mixed = jnp.concatenate([y_a, y_b, y_c], axis=-1)
    return mm_call(mixed, w_out, layer, 0, w_out.shape[-1], F32, res=h, name="out_proj")


def kernel(x, w_in, w_out, hgrn_lb, hgrn_norm, ssm_a_re, ssm_a_im, ssm_log_dt, ssm_b_re, ssm_b_im, ssm_c_re, ssm_c_im, ssm_d, ssm_w_glu, ssm_norm, att_rel_bias, att_norm, norm_mix, norm_ffn, peer_w_q, peer_keys1, peer_keys2, peer_u, peer_v, norm_final):
    bsz, seq, d = x.shape
    lb_all = jnp.cumsum(jax.nn.softmax(hgrn_lb.astype(F32), axis=0), axis=0)
    lb_all = lb_all - lb_all[0]
    h = x.reshape(bsz * seq, d)
    xn = rmsnorm_call(h, norm_mix[0], BF16)
    out = None
    for layer in range(DEPTH):
        tables = _s5_prep(ssm_a_re[layer], ssm_a_im[layer], ssm_log_dt[layer], ssm_b_re[layer],
                          ssm_b_im[layer], ssm_c_re[layer], ssm_c_im[layer])
        h = mixer_call(xn, h, w_in, w_out, layer, lb_all[layer], hgrn_norm[layer], tables,
                       ssm_d[layer], ssm_w_glu[layer], ssm_norm[layer], att_rel_bias[layer],
                       att_norm[layer], bsz)
        ffn = peer_ffn_call(h, norm_ffn[layer], peer_w_q, layer, peer_keys1[layer], peer_keys2[layer],
                            peer_u, peer_v)
        if layer + 1 < DEPTH:
            h, xn = add_rmsnorm_call(h, ffn, norm_mix[layer + 1], BF16, keep_sum=True)
        else:
            out = add_rmsnorm_call(h, ffn, norm_final, F32, keep_sum=False)
    return out.reshape(bsz, seq, d)
```

```python
import functools
import math

import jax
import jax.numpy as jnp
import numpy as np
from jax import lax
from jax.experimental import pallas as pl
from jax.experimental.pallas import tpu as pltpu

F32 = jnp.float32
BF16 = jnp.bfloat16

EPS = 1e-6
DEPTH = 2
CHUNK = 64
D_HGRN = 1024
HGRN_HEADS = 8
HGRN_HEAD_DIM = D_HGRN // HGRN_HEADS
HGRN_TILE = 128
D_SSM = 1024
SSM_GROUP = 16
SSM_GROUPS = D_SSM // SSM_GROUP
SSM_STATE = 64
SSM_CHUNK = 16
SSM_GB = 8
D_ATT = 2048
ATT_HEADS = 16
ATT_HEAD_DIM = D_ATT // ATT_HEADS
LEFT_CHUNKS = 8
MAX_REL = 256
ATT_TQ = 256
ATT_WIN = ATT_TQ + LEFT_CHUNKS * CHUNK
PEER_HEADS = 8
PEER_KEYS = 128
PEER_QDIM = 256
PEER_TOPK = 16

VMEM_LIMIT = 56 * 1024 * 1024

NT_DIMS = (((1,), (1,)), ((), ()))
TN_DIMS = (((0,), (0,)), ((), ()))


def _params(*sem):
    return pltpu.CompilerParams(dimension_semantics=sem, vmem_limit_bytes=VMEM_LIMIT)


def _rms(x, g):
    ms = jnp.mean(x * x, axis=-1, keepdims=True)
    return x * lax.rsqrt(ms + EPS) * g


def _rmsnorm_body(x_ref, g_ref, o_ref):
    o_ref[...] = _rms(x_ref[...], g_ref[...]).astype(o_ref.dtype)


def _add_rmsnorm_body(a_ref, b_ref, g_ref, *out_refs):
    s = a_ref[...] + b_ref[...]
    if len(out_refs) == 2:
        out_refs[0][...] = s
    out_refs[-1][...] = _rms(s, g_ref[...]).astype(out_refs[-1].dtype)


def rmsnorm_call(x, g, out_dtype, tm=256):
    t, d = x.shape
    row = pl.BlockSpec((tm, d), lambda i: (i, 0))
    return pl.pallas_call(
        _rmsnorm_body,
        grid=(t // tm,),
        in_specs=[row, pl.BlockSpec((1, d), lambda i: (0, 0))],
        out_specs=row,
        out_shape=jax.ShapeDtypeStruct((t, d), out_dtype),
        compiler_params=_params("parallel"),
        name="rmsnorm",
    )(x, g.reshape(1, d).astype(F32))


def add_rmsnorm_call(a, b, g, out_dtype, keep_sum, tm=256):
    t, d = a.shape
    row = pl.BlockSpec((tm, d), lambda i: (i, 0))
    out_specs = [row, row] if keep_sum else row
    normed = jax.ShapeDtypeStruct((t, d), out_dtype)
    out_shape = [jax.ShapeDtypeStruct((t, d), F32), normed] if keep_sum else normed
    return pl.pallas_call(
        _add_rmsnorm_body,
        grid=(t // tm,),
        in_specs=[row, row, pl.BlockSpec((1, d), lambda i: (0, 0))],
        out_specs=out_specs,
        out_shape=out_shape,
        compiler_params=_params("parallel"),
        name="add_rmsnorm",
    )(a, b, g.reshape(1, d).astype(F32))


def _mm_body(a_ref, w_ref, o_ref):
    w = w_ref[...].astype(BF16)
    o_ref[...] = jnp.dot(a_ref[...], w, preferred_element_type=F32).astype(o_ref.dtype)


def _mm_res_body(a_ref, w_ref, r_ref, o_ref):
    acc = jnp.dot(a_ref[...], w_ref[...].astype(BF16), preferred_element_type=F32)
    o_ref[...] = (r_ref[...] + acc).astype(o_ref.dtype)


def mm_call(a, w, layer, col0, n, out_dtype, res=None, tm=1024, tn=512, name="mm"):
    m, k = a.shape
    tm, tn = min(tm, m), min(tn, n)
    c0 = col0 // tn
    assert col0 % tn == 0 and n % tn == 0
    in_specs = [pl.BlockSpec((tm, k), lambda i, j: (i, 0)),
                pl.BlockSpec((None, k, tn), lambda i, j: (layer, 0, c0 + j))]
    args = [a, w]
    body = _mm_body
    if res is not None:
        in_specs.append(pl.BlockSpec((tm, tn), lambda i, j: (i, j)))
        args.append(res)
        body = _mm_res_body
    return pl.pallas_call(
        body,
        grid=(m // tm, n // tn),
        in_specs=in_specs,
        out_specs=pl.BlockSpec((tm, tn), lambda i, j: (i, j)),
        out_shape=jax.ShapeDtypeStruct((m, n), out_dtype),
        compiler_params=_params("parallel", "arbitrary"),
        name=name,
    )(*args)


def _hgrn_consts():
    c = HGRN_TILE
    r = np.arange(c)
    t, i = r[:, None], r[None, :]
    mats_q, mats_k, masks = [], [], []
    w = c // 2
    while w >= 1:
        seg = (r // (2 * w)) * (2 * w)
        second = (r % (2 * w)) >= w
        mats_q.append(second[:, None] & (i >= (seg + w)[:, None]) & (i <= t))
        mats_k.append(~second[:, None] & (i >= t + 1) & (i <= (seg + w - 1)[:, None]))
        masks.append((seg[:, None] == seg[None, :]) & second[:, None] & ~second[None, :])
        w //= 2
    masks.append(t == i)
    mats = [i <= t, i > t] + mats_q + mats_k
    return (jnp.asarray(np.stack(mats).astype(np.float32), BF16),
            jnp.asarray(np.stack(masks).astype(np.float32), F32), len(mats_q))


def _hgrn_body(q_ref, f_ref, i_ref, g_ref, am_ref, mask_ref, prm_ref, o_ref,
               st_ref, qt_ref, kt_ref, *, levels):
    hd = HGRN_HEAD_DIM

    @pl.when(pl.program_id(1) == 0)
    def _():
        st_ref[...] = jnp.zeros_like(st_ref)

    z = f_ref[...]
    log_lb, log_1m_lb, one_m_lb, norm_g = prm_ref[0:1], prm_ref[1:2], prm_ref[2:3], prm_ref[3:4]
    e = jnp.exp(-jnp.abs(z))
    log_sig = jnp.minimum(z, 0.0) - jnp.log1p(e)
    cc = log_1m_lb + log_sig
    log_f = jnp.maximum(log_lb, cc) + jnp.log1p(jnp.exp(-jnp.abs(log_lb - cc)))
    k = one_m_lb * jnp.where(z >= 0.0, e, 1.0) / (1.0 + e)
    q = q_ref[...]
    lf_hi = log_f.astype(BF16)
    lf_lo = (log_f - lf_hi.astype(F32)).astype(BF16)

    def exponent(m):
        a = am_ref[m]
        return jnp.dot(a, lf_hi, preferred_element_type=F32) + jnp.dot(a, lf_lo, preferred_element_type=F32)

    eb = jnp.exp(exponent(0))
    decay_all = eb[HGRN_TILE - 1:HGRN_TILE, :]
    qt_ref[0] = (q * eb).astype(BF16)
    kt_ref[0] = (k * jnp.exp(exponent(1))).astype(BF16)
    for lv in range(levels):
        qt_ref[1 + lv] = (q * jnp.exp(exponent(2 + lv))).astype(BF16)
        kt_ref[1 + lv] = (k * jnp.exp(exponent(2 + levels + lv))).astype(BF16)
    qt_ref[1 + levels] = q.astype(BF16)
    kt_ref[1 + levels] = k.astype(BF16)

    for h in range(HGRN_HEADS):
        hs = slice(h * hd, (h + 1) * hd)
        scores = None
        for lv in range(levels + 1):
            sc = lax.dot_general(qt_ref[1 + lv, :, hs], kt_ref[1 + lv, :, hs], NT_DIMS,
                                 preferred_element_type=F32) * mask_ref[lv]
            scores = sc if scores is None else scores + sc
        v = i_ref[:, hs].astype(BF16)
        st = st_ref[h]
        o = jnp.dot(scores.astype(BF16), v, preferred_element_type=F32)
        o = o + lax.dot_general(qt_ref[0, :, hs], st.astype(BF16), NT_DIMS, preferred_element_type=F32)
        st_ref[h] = st * decay_all[:, hs] + lax.dot_general(v, kt_ref[0, :, hs], TN_DIMS,
                                                            preferred_element_type=F32)
        o = o * lax.rsqrt(jnp.mean(o * o, axis=-1, keepdims=True) + EPS) * norm_g[:, hs]
        o_ref[:, hs] = (o * jax.nn.silu(g_ref[:, hs])).astype(o_ref.dtype)


def hgrn_call(proj, lb, norm_g, bsz):
    t = proj.shape[0]
    nblk = t // bsz // HGRN_TILE
    am, masks, levels = _hgrn_consts()
    lb = lb.astype(F32)
    prm = jnp.stack([jnp.log(lb), jnp.log1p(-lb), 1.0 - lb, norm_g.astype(F32)])
    col = lambda c: pl.BlockSpec((HGRN_TILE, D_HGRN), lambda b, n: (b * nblk + n, c))
    whole = lambda a: pl.BlockSpec(a.shape, lambda b, n: (0,) * a.ndim)
    return pl.pallas_call(
        functools.partial(_hgrn_body, levels=levels),
        grid=(bsz, nblk),
        in_specs=[col(0), col(1), col(2), col(3), whole(am), whole(masks), whole(prm)],
        out_specs=pl.BlockSpec((HGRN_TILE, D_HGRN), lambda b, n: (b * nblk + n, 0)),
        out_shape=jax.ShapeDtypeStruct((t, D_HGRN), BF16),
        scratch_shapes=[pltpu.VMEM((HGRN_HEADS, HGRN_HEAD_DIM, HGRN_HEAD_DIM), F32),
                        pltpu.VMEM((levels + 2, HGRN_TILE, D_HGRN), BF16),
                        pltpu.VMEM((levels + 2, HGRN_TILE, D_HGRN), BF16)],
        compiler_params=_params("parallel", "arbitrary"),
        name="hgrn2",
    )(proj, proj, proj, proj, am, masks, prm)


def _s5_prep(a_re, a_im, log_dt, b_re, b_im, c_re, c_im):
    ln, gb, hp = SSM_CHUNK, SSM_GB, lax.Precision.HIGHEST
    a = lax.complex(a_re.astype(F32), a_im.astype(F32))
    adt = a * jnp.exp(log_dt.astype(F32))[:, None]
    a_bar = jnp.exp(adt)
    b_bar = ((a_bar - 1.0) / a)[..., None] * lax.complex(b_re.astype(F32), b_im.astype(F32))
    c_mat = lax.complex(c_re.astype(F32), c_im.astype(F32))
    steps = jnp.arange(ln + 1, dtype=F32)
    pw = jnp.exp(adt[:, None, :] * steps[None, :, None])
    g, p, m = b_bar.shape
    nb = g // gb
    kern = jnp.einsum('gmp,gtp,gpn->gtmn', c_mat, pw[:, :ln], b_bar, precision=hp).real
    same = jnp.asarray(np.eye(gb, dtype=bool))[None, None, :, None, :, None]
    lag_t = kern.reshape(nb, gb, ln, m, m).transpose(0, 2, 1, 4, 3)
    lag_blocks = jnp.where(same, lag_t[:, :, :, :, None, :], 0).astype(BF16).reshape(nb, ln, gb * m, gb * m)
    back = jnp.exp(adt[:, None, :] * (ln - 1 - steps[:ln])[None, :, None])
    inc = back[:, :, None, :] * b_bar.transpose(0, 2, 1)[:, None, :, :]
    inc = jnp.concatenate([inc.real, inc.imag], axis=-1)
    inc = inc.reshape(nb, gb, ln, m, 2 * p).transpose(0, 2, 1, 3, 4).reshape(nb, ln * gb * m, 2 * p)
    out = c_mat.transpose(0, 2, 1)[:, :, None, :] * pw[:, 1:ln + 1].transpose(0, 2, 1)[:, :, :, None]
    out = jnp.concatenate([out.real, -out.imag], axis=1).reshape(nb, gb * 2 * p, ln * m)
    cp = jnp.exp(adt[:, None, :] * (ln * jnp.arange(9, dtype=F32))[None, :, None])
    rr = jnp.concatenate([cp.real, cp.real], axis=-1)
    ii = jnp.concatenate([-cp.imag, cp.imag], axis=-1)
    lanes = lambda x: x.reshape(nb, gb, 9, 2 * p).transpose(0, 2, 1, 3).reshape(nb, 9, gb * 2 * p)
    rr, ii = lanes(rr), lanes(ii)
    sel = np.array([1, 2, 4, 8])
    col = np.arange(ln * gb * m)
    expand = (np.arange(ln * m)[:, None] == ((col // (gb * m)) * m + col % m)[None, :]).astype(np.float32)
    return (lag_blocks, inc.astype(BF16), out.astype(BF16), jnp.asarray(expand, BF16),
            rr[:, :8], ii[:, :8], rr[:, sel], ii[:, sel])


def _s5_body(u_ref, lag_ref, inc_ref, out_ref, exp_ref, prr_ref, pii_ref, srr_ref, sii_ref, y_ref,
             d_ref, x_ref, bt_ref, mb_ref, mc_ref):
    ln = SSM_CHUNK
    lanes = SSM_GB * SSM_GROUP
    width = SSM_GB * 2 * SSM_STATE

    @pl.when(pl.program_id(1) == 0)
    def _():
        bt_ref[...] = jnp.zeros_like(bt_ref)
        for lp in range(ln):
            for l in range(lp, ln):
                bt_ref[lp * lanes:(lp + 1) * lanes, l * lanes:(l + 1) * lanes] = lag_ref[0, l - lp]
        r = lax.broadcasted_iota(jnp.int32, mb_ref.shape, 0)
        c = lax.broadcasted_iota(jnp.int32, mb_ref.shape, 1)
        own = ((r // SSM_GROUP) % SSM_GB) == (c // (2 * SSM_STATE))
        mb_ref[...] = jnp.where(own, jnp.concatenate([inc_ref[0]] * SSM_GB, axis=1), jnp.zeros((), BF16))
        r = lax.broadcasted_iota(jnp.int32, mc_ref.shape, 0)
        c = lax.broadcasted_iota(jnp.int32, mc_ref.shape, 1)
        own = (r // (2 * SSM_STATE)) == ((c % lanes) // SSM_GROUP)
        spread = jnp.dot(out_ref[0], exp_ref[...], preferred_element_type=F32).astype(BF16)
        mc_ref[...] = jnp.where(own, spread, jnp.zeros((), BF16))

    u = jnp.concatenate([u_ref[:, l, :].astype(BF16) for l in range(ln)], axis=1)
    d_ref[...] = jnp.dot(u, mb_ref[...], preferred_element_type=F32)
    half = SSM_STATE
    lane = lax.broadcasted_iota(jnp.int32, (8, width), 1)
    low = (lane % (2 * half)) < half

    def cmul(rr, ii, zz):
        swapped = jnp.where(low, pltpu.roll(zz, width - half, axis=1), pltpu.roll(zz, half, axis=1))
        return rr * zz + ii * swapped

    row = lax.broadcasted_iota(jnp.int32, (8, width), 0)
    prr, pii, srr, sii = prr_ref[0], pii_ref[0], srr_ref[0], sii_ref[0]
    carry = jnp.zeros((8, width), F32)
    for tl in range(d_ref.shape[0] // 8):
        r0 = tl * 8
        pre = d_ref[r0:r0 + 8, :]
        for n, s in enumerate((1, 2, 4)):
            shifted = jnp.where(row >= s, pltpu.roll(pre, s, axis=0), 0.0)
            pre = pre + cmul(srr[n:n + 1], sii[n:n + 1], shifted)
        x_ref[r0:r0 + 8, :] = cmul(prr, pii, carry) + jnp.where(row >= 1, pltpu.roll(pre, 1, axis=0), 0.0)
        carry = cmul(srr[3:4], sii[3:4], carry) + jnp.broadcast_to(pre[7:8, :], carry.shape)
    y = jnp.dot(u, bt_ref[...], preferred_element_type=F32)
    y = y + jnp.dot(x_ref[...].astype(BF16), mc_ref[...], preferred_element_type=F32)
    for l in range(ln):
        y_ref[:, l, :] = y[:, l * lanes:(l + 1) * lanes]


def _s5_post_body(y_ref, u_ref, prm_ref, w_ref, o_ref):
    y = y_ref[...] + prm_ref[0:1] * u_ref[...]
    y = jax.nn.gelu(y, approximate=True)
    gate = jnp.dot(y.astype(BF16), w_ref[...], preferred_element_type=F32)
    o_ref[...] = _rms(y * jax.nn.sigmoid(gate), prm_ref[1:2]).astype(o_ref.dtype)


def s5_call(u, tables, d_skip, w_glu, norm_g, bsz, tm=512):
    t = u.shape[0]
    rows = t // SSM_CHUNK
    lanes = SSM_GB * SSM_GROUP
    blk = pl.BlockSpec((rows // bsz, SSM_CHUNK, lanes), lambda g, b: (b, 0, g))
    tab = lambda a: (pl.BlockSpec((1,) + a.shape[1:], lambda g, b: (g,) + (0,) * (a.ndim - 1)) if a.ndim > 2
                     else pl.BlockSpec(a.shape, lambda g, b: (0, 0)))
    width = SSM_GB * 2 * SSM_STATE
    state = pltpu.VMEM((rows // bsz, width), F32)
    y = pl.pallas_call(
        _s5_body,
        grid=(SSM_GROUPS // SSM_GB, bsz),
        in_specs=[blk] + [tab(a) for a in tables],
        out_specs=blk,
        out_shape=jax.ShapeDtypeStruct((rows, SSM_CHUNK, D_SSM), F32),
        scratch_shapes=[state, state, pltpu.VMEM((SSM_CHUNK * lanes, SSM_CHUNK * lanes), BF16),
                        pltpu.VMEM((SSM_CHUNK * lanes, width), BF16), pltpu.VMEM((width, SSM_CHUNK * lanes), BF16)],
        compiler_params=_params("parallel", "arbitrary"),
        name="s5_scan",
    )(u.reshape(rows, SSM_CHUNK, D_SSM), *tables)
    tm = min(tm, t)
    prm = jnp.stack([d_skip.astype(F32), norm_g.astype(F32)])
    row = pl.BlockSpec((tm, D_SSM), lambda i: (i, 0))
    return pl.pallas_call(
        _s5_post_body,
        grid=(t // tm,),
        in_specs=[row, row, pl.BlockSpec((2, D_SSM), lambda i: (0, 0)),
                  pl.BlockSpec((D_SSM, D_SSM), lambda i: (0, 0))],
        out_specs=row,
        out_shape=jax.ShapeDtypeStruct((t, D_SSM), BF16),
        compiler_params=_params("parallel"),
        name="s5_post",
    )(y.reshape(t, D_SSM), u, prm, w_glu.astype(BF16))


def _att_bias(rel_bias):
    qi = np.arange(ATT_TQ)[:, None]
    kj = np.arange(ATT_WIN)[None, :]
    back = kj // CHUNK - qi // CHUNK
    valid = (back >= 0) & (back <= LEFT_CHUNKS)
    rb = rel_bias.astype(F32)
    nh = rb.shape[0]
    shift = LEFT_CHUNKS * CHUNK
    period = ATT_TQ + ATT_WIN - 1
    n_mid = ATT_WIN - (shift - MAX_REL)
    assert shift >= MAX_REL and n_mid <= 2 * MAX_REL + 1
    desc = rb[:, ::-1]
    far = lambda n: jnp.broadcast_to(desc[:, :1], (nh, n))
    g = jnp.concatenate([far(shift - MAX_REL), desc[:, :n_mid], far(ATT_TQ - 1)], axis=1)
    flat = jnp.tile(g, (1, ATT_TQ))[:, :ATT_TQ * (period - 1)]
    toeplitz = flat.reshape(nh, ATT_TQ, period - 1)[:, :, :ATT_WIN]
    return jnp.where(valid[None], toeplitz, -1e30)


def _att_body(q_ref, k_ref, v_ref, b_ref, o_ref):
    scale = ATT_HEAD_DIM ** -0.5
    lead = ATT_WIN // ATT_TQ - 1
    for i in range(q_ref.shape[1] // ATT_TQ):
        q0 = i * ATT_TQ
        k0 = max(0, q0 - (ATT_WIN - ATT_TQ))
        nk = q0 + ATT_TQ - k0
        bias = b_ref[0] if i >= lead else b_ref[0, :, ATT_WIN - nk:]
        s = lax.dot_general(q_ref[0, q0:q0 + ATT_TQ, :], k_ref[0, k0:k0 + nk, :], NT_DIMS,
                            preferred_element_type=F32) * scale + bias
        p = jnp.exp(s - jnp.max(s, axis=-1, keepdims=True))
        l = jnp.sum(p, axis=-1, keepdims=True)
        o_ref[0, q0:q0 + ATT_TQ, :] = jnp.dot(p.astype(BF16), v_ref[0, k0:k0 + nk, :],
                                              preferred_element_type=F32) / l


def attention_call(qkv, rel_bias, norm_g, bsz):
    t = qkv.shape[0]
    seq = t // bsz
    x = qkv.reshape(bsz, seq, 3 * D_ATT)
    bias = _att_bias(rel_bias)
    head = lambda c: pl.BlockSpec((1, seq, ATT_HEAD_DIM), lambda b, h: (b, 0, c * ATT_HEADS + h))
    o = pl.pallas_call(
        _att_body,
        grid=(bsz, ATT_HEADS),
        in_specs=[head(0), head(1), head(2), pl.BlockSpec((1, ATT_TQ, ATT_WIN), lambda b, h: (h, 0, 0))],
        out_specs=head(0),
        out_shape=jax.ShapeDtypeStruct((bsz, seq, D_ATT), F32),
        compiler_params=_params("parallel", "parallel"),
        name="chunk_attention",
    )(x, x, x, bias)
    return rmsnorm_call(o.reshape(t, D_ATT), norm_g, BF16)


def _top_rows(x, k):
    vals, where_ = [], []
    idx = lax.broadcasted_iota(jnp.int32, x.shape, 0).astype(F32)
    for _ in range(k):
        m = jnp.max(x, axis=0, keepdims=True)
        first = jnp.min(jnp.where(x == m, idx, float(x.shape[0])), axis=0, keepdims=True)
        x = jnp.where(idx == first, -jnp.inf, x)
        vals.append(m)
        where_.append(first)
    return vals, where_


def _peer_topk_body(q_ref, k1_ref, k2_ref, n1_ref, c1_ref, rk_ref, e2_ref):
    half = PEER_QDIM // 2
    k = PEER_TOPK
    for h in range(PEER_HEADS):
        base = h * PEER_QDIM
        s1 = lax.dot_general(k1_ref[...], q_ref[:, base:base + half], NT_DIMS, preferred_element_type=F32)
        s2 = lax.dot_general(k2_ref[...], q_ref[:, base + half:base + PEER_QDIM], NT_DIMS,
                             preferred_element_type=F32)
        v1, at1 = _top_rows(s1, k)
        v2, at2 = _top_rows(s2, k)
        v2 = jnp.concatenate(v2, axis=0)
        cand = jnp.concatenate([v1[a] + v2[:k // (a + 1)] for a in range(k)], axis=0)
        top, _ = _top_rows(cand, k)
        tau = top[k - 1]
        z = jnp.ones_like(tau)
        for t in top[1:]:
            z = z + jnp.exp(t - top[0])
        idx = lax.broadcasted_iota(jnp.int32, s1.shape, 0).astype(F32)
        count = jnp.zeros_like(s1)
        rank = jnp.full_like(s2, float(k))
        for a in range(k):
            n_a = jnp.sum(jnp.where(v1[a] + v2 >= tau, 1.0, 0.0), axis=0, keepdims=True)
            count = jnp.where(idx == at1[a], n_a, count)
            rank = jnp.where(idx == at2[a], float(a), rank)
        n1_ref[h] = count
        c1_ref[h] = jnp.exp(s1 - v1[0]) / z
        rk_ref[h] = rank.astype(rk_ref.dtype)
        e2_ref[h] = jnp.exp(s2 - v2[0:1]).astype(e2_ref.dtype)


def peer_topk_call(q, keys1, keys2, tm=256):
    t = q.shape[0]
    hk = (PEER_HEADS, PEER_KEYS, t)
    blk = pl.BlockSpec((PEER_HEADS, PEER_KEYS, tm), lambda i: (0, 0, i))
    return pl.pallas_call(
        _peer_topk_body,
        grid=(t // tm,),
        in_specs=[pl.BlockSpec((tm, PEER_HEADS * PEER_QDIM), lambda i: (i, 0)),
                  pl.BlockSpec((PEER_KEYS, PEER_QDIM // 2), lambda i: (0, 0)),
                  pl.BlockSpec((PEER_KEYS, PEER_QDIM // 2), lambda i: (0, 0))],
        out_specs=[blk, blk, blk, blk],
        out_shape=[jax.ShapeDtypeStruct(hk, F32), jax.ShapeDtypeStruct(hk, F32),
                   jax.ShapeDtypeStruct(hk, BF16), jax.ShapeDtypeStruct(hk, BF16)],
        compiler_params=_params("parallel"),
        name="peer_topk",
    )(q, keys1, keys2)


def _peer_gate_body(x_ref, u_ref, n1_ref, c1_ref, rk_ref, e2_ref, w_ref, *, te):
    j = pl.program_id(1)
    ht = lax.dot_general(u_ref[...].astype(BF16), x_ref[...], NT_DIMS, preferred_element_type=F32)
    for s in range(te // PEER_KEYS):
        i1 = j * (te // PEER_KEYS) + s
        gate = None
        for h in range(PEER_HEADS):
            count = n1_ref[h, pl.ds(i1, 1), :].astype(BF16)
            factor = c1_ref[h, pl.ds(i1, 1), :].astype(BF16)
            term = jnp.where(rk_ref[h] < count, e2_ref[h], jnp.zeros((), BF16)) * factor
            gate = term if gate is None else gate + term
        act = jax.nn.gelu(ht[s * PEER_KEYS:(s + 1) * PEER_KEYS], approximate=True).astype(BF16)
        w_ref[:, s * PEER_KEYS:(s + 1) * PEER_KEYS] = jnp.transpose(gate * act)


def peer_gate_call(hn, u, layer, n1, c1, rk, e2, tm=1024, te=256):
    t, d = hn.shape
    e = u.shape[1]
    tm = min(tm, t)
    hk = pl.BlockSpec((PEER_HEADS, PEER_KEYS, tm), lambda i, j: (0, 0, i))
    return pl.pallas_call(
        functools.partial(_peer_gate_body, te=te),
        grid=(t // tm, e // te),
        in_specs=[pl.BlockSpec((tm, d), lambda i, j: (i, 0)),
                  pl.BlockSpec((None, te, d), lambda i, j: (layer, j, 0)),
                  hk, hk, hk, hk],
        out_specs=pl.BlockSpec((tm, te), lambda i, j: (i, j)),
        out_shape=jax.ShapeDtypeStruct((t, e), BF16),
        compiler_params=_params("parallel", "arbitrary"),
        name="peer_gate",
    )(hn, u, n1, c1, rk, e2)


def _mm_acc_body(a_ref, w_ref, o_ref):
    @pl.when(pl.program_id(2) == 0)
    def _():
        o_ref[...] = jnp.zeros_like(o_ref)

    o_ref[...] += jnp.dot(a_ref[...], w_ref[...].astype(BF16), preferred_element_type=F32)


def mm_acc_call(a, w, layer, tm=2048, tn=1024, tk=1024, name="mm_acc"):
    m, k = a.shape
    n = w.shape[2]
    tm, tn, tk = min(tm, m), min(tn, n), min(tk, k)
    return pl.pallas_call(
        _mm_acc_body,
        grid=(m // tm, n // tn, k // tk),
        in_specs=[pl.BlockSpec((tm, tk), lambda i, j, l: (i, l)),
                  pl.BlockSpec((None, tk, tn), lambda i, j, l: (layer, l, j))],
        out_specs=pl.BlockSpec((tm, tn), lambda i, j, l: (i, j)),
        out_shape=jax.ShapeDtypeStruct((m, n), F32),
        compiler_params=_params("parallel", "parallel", "arbitrary"),
        name=name,
    )(a, w)


def peer_ffn_call(h, norm_g, w_q, layer, keys1, keys2, u, v):
    hn = rmsnorm_call(h, norm_g, BF16)
    q = mm_call(hn, w_q, layer, 0, w_q.shape[-1], BF16, name="peer_q")
    n1, c1, rk, e2 = peer_topk_call(q, keys1.astype(BF16), keys2.astype(BF16))
    w = peer_gate_call(hn, u, layer, n1, c1, rk, e2)
    return mm_acc_call(w, v, layer, name="peer_out")


def mixer_call(xn, h, w_in, w_out, layer, lb, hgrn_norm, s5_tables, ssm_d, ssm_w_glu, ssm_norm, rel_bias, att_norm,
               bsz):
    n_a = 4 * D_HGRN
    proj_a = mm_call(xn, w_in, layer, 0, n_a, F32, name="in_proj_a")
    proj_b = mm_call(xn, w_in, layer, n_a, D_SSM, F32, name="in_proj_b")
    proj_c = mm_call(xn, w_in, layer, n_a + D_SSM, 3 * D_ATT, BF16, name="in_proj_c")
    y_a = hgrn_call(proj_a, lb, hgrn_norm, bsz)
    y_b = s5_call(proj_b, s5_tables, ssm_d, ssm_w_glu, ssm_norm, bsz)
    y_c = attention_call(proj_c, rel_bias, att_norm, bsz)
    mixed = jnp.concatenate([y_a, y_b, y_c], axis=-1)
    return mm_call(mixed, w_out, layer, 0, w_out.shape[-1], F32, res=h, name="out_proj")


def kernel(x, w_in, w_out, hgrn_lb, hgrn_norm, ssm_a_re, ssm_a_im, ssm_log_dt, ssm_b_re, ssm_b_im, ssm_c_re, ssm_c_im, ssm_d, ssm_w_glu, ssm_norm, att_rel_bias, att_norm, norm_mix, norm_ffn, peer_w_q, peer_keys1, peer_keys2, peer_u, peer_v, norm_final):
    bsz, seq, d = x.shape
    lb_all = jnp.cumsum(jax.nn.softmax(hgrn_lb.astype(F32), axis=0), axis=0)
    lb_all = lb_all - lb_all[0]
    h = x.reshape(bsz * seq, d)
    xn = rmsnorm_call(h, norm_mix[0], BF16)
    out = None
    for layer in range(DEPTH):
        tables = _s5_prep(ssm_a_re[layer], ssm_a_im[layer], ssm_log_dt[layer], ssm_b_re[layer],
                          ssm_b_im[layer], ssm_c_re[layer], ssm_c_im[layer])
        h = mixer_call(xn, h, w_in, w_out, layer, lb_all[layer], hgrn_norm[layer], tables,
                       ssm_d[layer], ssm_w_glu[layer], ssm_norm[layer], att_rel_bias[layer],
                       att_norm[layer], bsz)
        ffn = peer_ffn_call(h, norm_ffn[layer], peer_w_q, layer, peer_keys1[layer], peer_keys2[layer],
                            peer_u, peer_v)
        if layer + 1 < DEPTH:
            h, xn = add_rmsnorm_call(h, ffn, norm_mix[layer + 1], BF16, keep_sum=True)
        else:
            out = add_rmsnorm_call(h, ffn, norm_final, F32, keep_sum=False)
    return out.reshape(bsz, seq, d)
```

```python
import functools
import math

import jax
import jax.numpy as jnp
import numpy as np
from jax import lax
from jax.experimental import pallas as pl
from jax.experimental.pallas import tpu as pltpu

F32 = jnp.float32
BF16 = jnp.bfloat16

EPS = 1e-6
DEPTH = 2
CHUNK = 64
D_HGRN = 1024
HGRN_HEADS = 8
HGRN_HEAD_DIM = D_HGRN // HGRN_HEADS
HGRN_TILE = 128
D_SSM = 1024
SSM_GROUP = 16
SSM_GROUPS = D_SSM // SSM_GROUP
SSM_STATE = 64
SSM_CHUNK = 16
SSM_GB = 8
D_ATT = 2048
ATT_HEADS = 16
ATT_HEAD_DIM = D_ATT // ATT_HEADS
LEFT_CHUNKS = 8
MAX_REL = 256
ATT_TQ = 256
ATT_WIN = ATT_TQ + LEFT_CHUNKS * CHUNK
PEER_HEADS = 8
PEER_KEYS = 128
PEER_QDIM = 256
PEER_TOPK = 16

VMEM_LIMIT = 56 * 1024 * 1024

NT_DIMS = (((1,), (1,)), ((), ()))
TN_DIMS = (((0,), (0,)), ((), ()))


def _params(*sem):
    return pltpu.CompilerParams(dimension_semantics=sem, vmem_limit_bytes=VMEM_LIMIT)


def _rms(x, g):
    ms = jnp.mean(x * x, axis=-1, keepdims=True)
    return x * lax.rsqrt(ms + EPS) * g


def _rmsnorm_body(x_ref, g_ref, o_ref):
    o_ref[...] = _rms(x_ref[...], g_ref[...]).astype(o_ref.dtype)


def _add_rmsnorm_body(a_ref, b_ref, g_ref, *out_refs):
    s = a_ref[...] + b_ref[...]
    if len(out_refs) == 2:
        out_refs[0][...] = s
    out_refs[-1][...] = _rms(s, g_ref[...]).astype(out_refs[-1].dtype)


def rmsnorm_call(x, g, out_dtype, tm=256):
    t, d = x.shape
    row = pl.BlockSpec((tm, d), lambda i: (i, 0))
    return pl.pallas_call(
        _rmsnorm_body,
        grid=(t // tm,),
        in_specs=[row, pl.BlockSpec((1, d), lambda i: (0, 0))],
        out_specs=row,
        out_shape=jax.ShapeDtypeStruct((t, d), out_dtype),
        compiler_params=_params("parallel"),
        name="rmsnorm",
    )(x, g.reshape(1, d).astype(F32))


def add_rmsnorm_call(a, b, g, out_dtype, keep_sum, tm=256):
    t, d = a.shape
    row = pl.BlockSpec((tm, d), lambda i: (i, 0))
    out_specs = [row, row] if keep_sum else row
    normed = jax.ShapeDtypeStruct((t, d), out_dtype)
    out_shape = [jax.ShapeDtypeStruct((t, d), F32), normed] if keep_sum else normed
    return pl.pallas_call(
        _add_rmsnorm_body,
        grid=(t // tm,),
        in_specs=[row, row, pl.BlockSpec((1, d), lambda i: (0, 0))],
        out_specs=out_specs,
        out_shape=out_shape,
        compiler_params=_params("parallel"),
        name="add_rmsnorm",
    )(a, b, g.reshape(1, d).astype(F32))


def _mm_body(a_ref, w_ref, o_ref):
    w = w_ref[...].astype(BF16)
    o_ref[...] = jnp.dot(a_ref[...], w, preferred_element_type=F32).astype(o_ref.dtype)


def _mm_res_body(a_ref, w_ref, r_ref, o_ref):
    acc = jnp.dot(a_ref[...], w_ref[...].astype(BF16), preferred_element_type=F32)
    o_ref[...] = (r_ref[...] + acc).astype(o_ref.dtype)


def mm_call(a, w, layer, col0, n, out_dtype, res=None, tm=1024, tn=512, name="mm"):
    m, k = a.shape
    tm, tn = min(tm, m), min(tn, n)
    c0 = col0 // tn
    assert col0 % tn == 0 and n % tn == 0
    in_specs = [pl.BlockSpec((tm, k), lambda i, j: (i, 0)),
                pl.BlockSpec((None, k, tn), lambda i, j: (layer, 0, c0 + j))]
    args = [a, w]
    body = _mm_body
    if res is not None:
        in_specs.append(pl.BlockSpec((tm, tn), lambda i, j: (i, j)))
        args.append(res)
        body = _mm_res_body
    return pl.pallas_call(
        body,
        grid=(m // tm, n // tn),
        in_specs=in_specs,
        out_specs=pl.BlockSpec((tm, tn), lambda i, j: (i, j)),
        out_shape=jax.ShapeDtypeStruct((m, n), out_dtype),
        compiler_params=_params("parallel", "arbitrary"),
        name=name,
    )(*args)


def _hgrn_consts():
    c = HGRN_TILE
    r = np.arange(c)
    t, i = r[:, None], r[None, :]
    mats_q, mats_k, masks = [], [], []
    w = c // 2
    while w >= 1:
        seg = (r // (2 * w)) * (2 * w)
        second = (r % (2 * w)) >= w
        mats_q.append(second[:, None] & (i >= (seg + w)[:, None]) & (i <= t))
        mats_k.append(~second[:, None] & (i >= t + 1) & (i <= (seg + w - 1)[:, None]))
        masks.append((seg[:, None] == seg[None, :]) & second[:, None] & ~second[None, :])
        w //= 2
    masks.append(t == i)
    mats = np.concatenate([i <= t, i > t] + mats_q + mats_k, axis=0).astype(np.float32)
    return (jnp.asarray(np.concatenate([mats, mats], axis=1), BF16),
            jnp.asarray(np.stack(masks).astype(np.float32), F32), len(mats_q))


def _hgrn_body(q_ref, f_ref, i_ref, g_ref, am_ref, mask_ref, prm_ref, o_ref,
               st_ref, qt_ref, kt_ref, ex_ref, *, levels):
    hd = HGRN_HEAD_DIM

    @pl.when(pl.program_id(1) == 0)
    def _():
        st_ref[...] = jnp.zeros_like(st_ref)

    z = f_ref[...]
    log_lb, log_1m_lb, one_m_lb, norm_g = prm_ref[0:1], prm_ref[1:2], prm_ref[2:3], prm_ref[3:4]
    e = jnp.exp(-jnp.abs(z))
    log_sig = jnp.minimum(z, 0.0) - jnp.log1p(e)
    cc = log_1m_lb + log_sig
    log_f = jnp.maximum(log_lb, cc) + jnp.log1p(jnp.exp(-jnp.abs(log_lb - cc)))
    k = one_m_lb * jnp.where(z >= 0.0, e, 1.0) / (1.0 + e)
    q = q_ref[...]
    lf_hi = log_f.astype(BF16)
    lf_lo = (log_f - lf_hi.astype(F32)).astype(BF16)
    tile = HGRN_TILE
    ex_ref[...] = jnp.dot(am_ref[...], jnp.concatenate([lf_hi, lf_lo], axis=0), preferred_element_type=F32)

    def decay(m):
        return jnp.exp(ex_ref[m * tile:(m + 1) * tile, :])

    eb = decay(0)
    decay_all = eb[tile - 1:tile, :]
    qt_ref[0] = (q * eb).astype(BF16)
    kt_ref[0] = (k * decay(1)).astype(BF16)
    for lv in range(levels):
        qt_ref[1 + lv] = (q * decay(2 + lv)).astype(BF16)
        kt_ref[1 + lv] = (k * decay(2 + levels + lv)).astype(BF16)
    qt_ref[1 + levels] = q.astype(BF16)
    kt_ref[1 + levels] = k.astype(BF16)

    for h in range(HGRN_HEADS):
        hs = slice(h * hd, (h + 1) * hd)
        scores = None
        for lv in range(levels + 1):
            sc = lax.dot_general(qt_ref[1 + lv, :, hs], kt_ref[1 + lv, :, hs], NT_DIMS,
                                 preferred_element_type=F32) * mask_ref[lv]
            scores = sc if scores is None else scores + sc
        v = i_ref[:, hs].astype(BF16)
        st = st_ref[h]
        o = jnp.dot(scores.astype(BF16), v, preferred_element_type=F32)
        o = o + lax.dot_general(qt_ref[0, :, hs], st.astype(BF16), NT_DIMS, preferred_element_type=F32)
        st_ref[h] = st * decay_all[:, hs] + lax.dot_general(v, kt_ref[0, :, hs], TN_DIMS,
                                                            preferred_element_type=F32)
        o = o * lax.rsqrt(jnp.mean(o * o, axis=-1, keepdims=True) + EPS) * norm_g[:, hs]
        o_ref[:, hs] = (o * jax.nn.silu(g_ref[:, hs])).astype(o_ref.dtype)


def hgrn_call(proj, lb, norm_g, bsz):
    t = proj.shape[0]
    nblk = t // bsz // HGRN_TILE
    am, masks, levels = _hgrn_consts()
    lb = lb.astype(F32)
    prm = jnp.stack([jnp.log(lb), jnp.log1p(-lb), 1.0 - lb, norm_g.astype(F32)])
    col = lambda c: pl.BlockSpec((HGRN_TILE, D_HGRN), lambda b, n: (b * nblk + n, c))
    whole = lambda a: pl.BlockSpec(a.shape, lambda b, n: (0,) * a.ndim)
    return pl.pallas_call(
        functools.partial(_hgrn_body, levels=levels),
        grid=(bsz, nblk),
        in_specs=[col(0), col(1), col(2), col(3), whole(am), whole(masks), whole(prm)],
        out_specs=pl.BlockSpec((HGRN_TILE, D_HGRN), lambda b, n: (b * nblk + n, 0)),
        out_shape=jax.ShapeDtypeStruct((t, D_HGRN), BF16),
        scratch_shapes=[pltpu.VMEM((HGRN_HEADS, HGRN_HEAD_DIM, HGRN_HEAD_DIM), F32),
                        pltpu.VMEM((levels + 2, HGRN_TILE, D_HGRN), BF16),
                        pltpu.VMEM((levels + 2, HGRN_TILE, D_HGRN), BF16),
                        pltpu.VMEM(((2 * levels + 2) * HGRN_TILE, D_HGRN), F32)],
        compiler_params=_params("parallel", "arbitrary"),
        name="hgrn2",
    )(proj, proj, proj, proj, am, masks, prm)


def _s5_prep(a_re, a_im, log_dt, b_re, b_im, c_re, c_im):
    ln, gb, hp = SSM_CHUNK, SSM_GB, lax.Precision.HIGHEST
    a = lax.complex(a_re.astype(F32), a_im.astype(F32))
    adt = a * jnp.exp(log_dt.astype(F32))[:, None]
    a_bar = jnp.exp(adt)
    b_bar = ((a_bar - 1.0) / a)[..., None] * lax.complex(b_re.astype(F32), b_im.astype(F32))
    c_mat = lax.complex(c_re.astype(F32), c_im.astype(F32))
    steps = jnp.arange(ln + 1, dtype=F32)
    pw = jnp.exp(adt[:, None, :] * steps[None, :, None])
    g, p, m = b_bar.shape
    nb = g // gb
    kern = jnp.einsum('gmp,gtp,gpn->gtmn', c_mat, pw[:, :ln], b_bar, precision=hp).real
    same = jnp.asarray(np.eye(gb, dtype=bool))[None, None, :, None, :, None]
    lag_t = kern.reshape(nb, gb, ln, m, m).transpose(0, 2, 1, 4, 3)
    lag_blocks = jnp.where(same, lag_t[:, :, :, :, None, :], 0).astype(BF16).reshape(nb, ln, gb * m, gb * m)
    back = jnp.exp(adt[:, None, :] * (ln - 1 - steps[:ln])[None, :, None])
    inc = back[:, :, None, :] * b_bar.transpose(0, 2, 1)[:, None, :, :]
    inc = jnp.concatenate([inc.real, inc.imag], axis=-1)
    inc = inc.reshape(nb, gb, ln, m, 2 * p).transpose(0, 2, 1, 3, 4).reshape(nb, ln * gb * m, 2 * p)
    out = c_mat.transpose(0, 2, 1)[:, :, None, :] * pw[:, 1:ln + 1].transpose(0, 2, 1)[:, :, :, None]
    out = jnp.concatenate([out.real, -out.imag], axis=1).reshape(nb, gb * 2 * p, ln * m)
    cp = jnp.exp(adt[:, None, :] * (ln * jnp.arange(9, dtype=F32))[None, :, None])
    rr = jnp.concatenate([cp.real, cp.real], axis=-1)
    ii = jnp.concatenate([-cp.imag, cp.imag], axis=-1)
    lanes = lambda x: x.reshape(nb, gb, 9, 2 * p).transpose(0, 2, 1, 3).reshape(nb, 9, gb * 2 * p)
    rr, ii = lanes(rr), lanes(ii)
    sel = np.array([1, 2, 4, 8])
    col = np.arange(ln * gb * m)
    expand = (np.arange(ln * m)[:, None] == ((col // (gb * m)) * m + col % m)[None, :]).astype(np.float32)
    return (lag_blocks, inc.astype(BF16), out.astype(BF16), jnp.asarray(expand, BF16),
            rr[:, :8], ii[:, :8], rr[:, sel], ii[:, sel])


def _s5_body(u_ref, lag_ref, inc_ref, out_ref, exp_ref, prr_ref, pii_ref, srr_ref, sii_ref, y_ref,
             d_ref, x_ref, bt_ref, mb_ref, mc_ref):
    ln = SSM_CHUNK
    lanes = SSM_GB * SSM_GROUP
    width = SSM_GB * 2 * SSM_STATE

    @pl.when(pl.program_id(1) == 0)
    def _():
        bt_ref[...] = jnp.zeros_like(bt_ref)
        for lp in range(ln):
            for l in range(lp, ln):
                bt_ref[lp * lanes:(lp + 1) * lanes, l * lanes:(l + 1) * lanes] = lag_ref[0, l - lp]
        r = lax.broadcasted_iota(jnp.int32, mb_ref.shape, 0)
        c = lax.broadcasted_iota(jnp.int32, mb_ref.shape, 1)
        own = ((r // SSM_GROUP) % SSM_GB) == (c // (2 * SSM_STATE))
        mb_ref[...] = jnp.where(own, jnp.concatenate([inc_ref[0]] * SSM_GB, axis=1), jnp.zeros((), BF16))
        r = lax.broadcasted_iota(jnp.int32, mc_ref.shape, 0)
        c = lax.broadcasted_iota(jnp.int32, mc_ref.shape, 1)
        own = (r // (2 * SSM_STATE)) == ((c % lanes) // SSM_GROUP)
        spread = jnp.dot(out_ref[0], exp_ref[...], preferred_element_type=F32).astype(BF16)
        mc_ref[...] = jnp.where(own, spread, jnp.zeros((), BF16))

    u = jnp.concatenate([u_ref[:, l, :].astype(BF16) for l in range(ln)], axis=1)
    d_ref[...] = jnp.dot(u, mb_ref[...], preferred_element_type=F32)
    half = SSM_STATE
    lane = lax.broadcasted_iota(jnp.int32, (8, width), 1)
    low = (lane % (2 * half)) < half

    def cmul(rr, ii, zz):
        swapped = jnp.where(low, pltpu.roll(zz, width - half, axis=1), pltpu.roll(zz, half, axis=1))
        return rr * zz + ii * swapped

    row = lax.broadcasted_iota(jnp.int32, (8, width), 0)
    prr, pii, srr, sii = prr_ref[0], pii_ref[0], srr_ref[0], sii_ref[0]
    carry = jnp.zeros((8, width), F32)
    for tl in range(d_ref.shape[0] // 8):
        r0 = tl * 8
        pre = d_ref[r0:r0 + 8, :]
        for n, s in enumerate((1, 2, 4)):
            shifted = jnp.where(row >= s, pltpu.roll(pre, s, axis=0), 0.0)
            pre = pre + cmul(srr[n:n + 1], sii[n:n + 1], shifted)
        x_ref[r0:r0 + 8, :] = cmul(prr, pii, carry) + jnp.where(row >= 1, pltpu.roll(pre, 1, axis=0), 0.0)
        carry = cmul(srr[3:4], sii[3:4], carry) + jnp.broadcast_to(pre[7:8, :], carry.shape)
    y = jnp.dot(u, bt_ref[...], preferred_element_type=F32)
    y = y + jnp.dot(x_ref[...].astype(BF16), mc_ref[...], preferred_element_type=F32)
    for l in range(ln):
        y_ref[:, l, :] = y[:, l * lanes:(l + 1) * lanes]


def _s5_post_body(y_ref, u_ref, prm_ref, w_ref, o_ref):
    y = y_ref[...] + prm_ref[0:1] * u_ref[...]
    y = jax.nn.gelu(y, approximate=True)
    gate = jnp.dot(y.astype(BF16), w_ref[...], preferred_element_type=F32)
    o_ref[...] = _rms(y * jax.nn.sigmoid(gate), prm_ref[1:2]).astype(o_ref.dtype)


def s5_call(u, tables, d_skip, w_glu, norm_g, bsz, tm=512):
    t = u.shape[0]
    rows = t // SSM_CHUNK
    lanes = SSM_GB * SSM_GROUP
    blk = pl.BlockSpec((rows // bsz, SSM_CHUNK, lanes), lambda g, b: (b, 0, g))
    tab = lambda a: (pl.BlockSpec((1,) + a.shape[1:], lambda g, b: (g,) + (0,) * (a.ndim - 1)) if a.ndim > 2
                     else pl.BlockSpec(a.shape, lambda g, b: (0, 0)))
    width = SSM_GB * 2 * SSM_STATE
    state = pltpu.VMEM((rows // bsz, width), F32)
    y = pl.pallas_call(
        _s5_body,
        grid=(SSM_GROUPS // SSM_GB, bsz),
        in_specs=[blk] + [tab(a) for a in tables],
        out_specs=blk,
        out_shape=jax.ShapeDtypeStruct((rows, SSM_CHUNK, D_SSM), F32),
        scratch_shapes=[state, state, pltpu.VMEM((SSM_CHUNK * lanes, SSM_CHUNK * lanes), BF16),
                        pltpu.VMEM((SSM_CHUNK * lanes, width), BF16), pltpu.VMEM((width, SSM_CHUNK * lanes), BF16)],
        compiler_params=_params("parallel", "arbitrary"),
        name="s5_scan",
    )(u.reshape(rows, SSM_CHUNK, D_SSM), *tables)
    tm = min(tm, t)
    prm = jnp.stack([d_skip.astype(F32), norm_g.astype(F32)])
    row = pl.BlockSpec((tm, D_SSM), lambda i: (i, 0))
    return pl.pallas_call(
        _s5_post_body,
        grid=(t // tm,),
        in_specs=[row, row, pl.BlockSpec((2, D_SSM), lambda i: (0, 0)),
                  pl.BlockSpec((D_SSM, D_SSM), lambda i: (0, 0))],
        out_specs=row,
        out_shape=jax.ShapeDtypeStruct((t, D_SSM), BF16),
        compiler_params=_params("parallel"),
        name="s5_post",
    )(y.reshape(t, D_SSM), u, prm, w_glu.astype(BF16))


def _att_bias(rel_bias):
    qi = np.arange(ATT_TQ)[:, None]
    kj = np.arange(ATT_WIN)[None, :]
    back = kj // CHUNK - qi // CHUNK
    valid = (back >= 0) & (back <= LEFT_CHUNKS)
    rb = rel_bias.astype(F32)
    nh = rb.shape[0]
    shift = LEFT_CHUNKS * CHUNK
    period = ATT_TQ + ATT_WIN - 1
    n_mid = ATT_WIN - (shift - MAX_REL)
    assert shift >= MAX_REL and n_mid <= 2 * MAX_REL + 1
    desc = rb[:, ::-1]
    far = lambda n: jnp.broadcast_to(desc[:, :1], (nh, n))
    g = jnp.concatenate([far(shift - MAX_REL), desc[:, :n_mid], far(ATT_TQ - 1)], axis=1)
    flat = jnp.tile(g, (1, ATT_TQ))[:, :ATT_TQ * (period - 1)]
    toeplitz = flat.reshape(nh, ATT_TQ, period - 1)[:, :, :ATT_WIN]
    return jnp.where(valid[None], toeplitz, -1e30)


def _att_body(q_ref, k_ref, v_ref, b_ref, o_ref):
    scale = ATT_HEAD_DIM ** -0.5
    lead = ATT_WIN // ATT_TQ - 1
    for i in range(q_ref.shape[1] // ATT_TQ):
        q0 = i * ATT_TQ
        k0 = max(0, q0 - (ATT_WIN - ATT_TQ))
        nk = q0 + ATT_TQ - k0
        bias = b_ref[0] if i >= lead else b_ref[0, :, ATT_WIN - nk:]
        s = lax.dot_general(q_ref[0, q0:q0 + ATT_TQ, :], k_ref[0, k0:k0 + nk, :], NT_DIMS,
                            preferred_element_type=F32) * scale + bias
        p = jnp.exp(s - jnp.max(s, axis=-1, keepdims=True))
        l = jnp.sum(p, axis=-1, keepdims=True)
        o_ref[0, q0:q0 + ATT_TQ, :] = jnp.dot(p.astype(BF16), v_ref[0, k0:k0 + nk, :],
                                              preferred_element_type=F32) / l


def attention_call(qkv, rel_bias, norm_g, bsz):
    t = qkv.shape[0]
    seq = t // bsz
    x = qkv.reshape(bsz, seq, 3 * D_ATT)
    bias = _att_bias(rel_bias)
    head = lambda c: pl.BlockSpec((1, seq, ATT_HEAD_DIM), lambda b, h: (b, 0, c * ATT_HEADS + h))
    o = pl.pallas_call(
        _att_body,
        grid=(bsz, ATT_HEADS),
        in_specs=[head(0), head(1), head(2), pl.BlockSpec((1, ATT_TQ, ATT_WIN), lambda b, h: (h, 0, 0))],
        out_specs=head(0),
        out_shape=jax.ShapeDtypeStruct((bsz, seq, D_ATT), F32),
        compiler_params=_params("parallel", "parallel"),
        name="chunk_attention",
    )(x, x, x, bias)
    return rmsnorm_call(o.reshape(t, D_ATT), norm_g, BF16)


def _top_rows(x, k):
    vals, where_ = [], []
    idx = lax.broadcasted_iota(jnp.int32, x.shape, 0).astype(F32)
    for _ in range(k):
        m = jnp.max(x, axis=0, keepdims=True)
        first = jnp.min(jnp.where(x == m, idx, float(x.shape[0])), axis=0, keepdims=True)
        x = jnp.where(idx == first, -jnp.inf, x)
        vals.append(m)
        where_.append(first)
    return vals, where_


def _peer_topk_body(q_ref, k1_ref, k2_ref, n1_ref, c1_ref, rk_ref, e2_ref):
    half = PEER_QDIM // 2
    k = PEER_TOPK
    for h in range(PEER_HEADS):
        base = h * PEER_QDIM
        s1 = lax.dot_general(k1_ref[...], q_ref[:, base:base + half], NT_DIMS, preferred_element_type=F32)
        s2 = lax.dot_general(k2_ref[...], q_ref[:, base + half:base + PEER_QDIM], NT_DIMS,
                             preferred_element_type=F32)
        v1, at1 = _top_rows(s1, k)
        v2, at2 = _top_rows(s2, k)
        v2 = jnp.concatenate(v2, axis=0)
        cand = jnp.concatenate([v1[a] + v2[:k // (a + 1)] for a in range(k)], axis=0)
        top, _ = _top_rows(cand, k)
        tau = top[k - 1]
        z = jnp.ones_like(tau)
        for t in top[1:]:
            z = z + jnp.exp(t - top[0])
        idx = lax.broadcasted_iota(jnp.int32, s1.shape, 0).astype(F32)
        count = jnp.zeros_like(s1)
        rank = jnp.full_like(s2, float(k))
        for a in range(k):
            n_a = jnp.sum(jnp.where(v1[a] + v2 >= tau, 1.0, 0.0), axis=0, keepdims=True)
            count = jnp.where(idx == at1[a], n_a, count)
            rank = jnp.where(idx == at2[a], float(a), rank)
        n1_ref[h] = count
        c1_ref[h] = jnp.exp(s1 - v1[0]) / z
        rk_ref[h] = rank.astype(rk_ref.dtype)
        e2_ref[h] = jnp.exp(s2 - v2[0:1]).astype(e2_ref.dtype)


def peer_topk_call(q, keys1, keys2, tm=256):
    t = q.shape[0]
    hk = (PEER_HEADS, PEER_KEYS, t)
    blk = pl.BlockSpec((PEER_HEADS, PEER_KEYS, tm), lambda i: (0, 0, i))
    return pl.pallas_call(
        _peer_topk_body,
        grid=(t // tm,),
        in_specs=[pl.BlockSpec((tm, PEER_HEADS * PEER_QDIM), lambda i: (i, 0)),
                  pl.BlockSpec((PEER_KEYS, PEER_QDIM // 2), lambda i: (0, 0)),
                  pl.BlockSpec((PEER_KEYS, PEER_QDIM // 2), lambda i: (0, 0))],
        out_specs=[blk, blk, blk, blk],
        out_shape=[jax.ShapeDtypeStruct(hk, F32), jax.ShapeDtypeStruct(hk, F32),
                   jax.ShapeDtypeStruct(hk, BF16), jax.ShapeDtypeStruct(hk, BF16)],
        compiler_params=_params("parallel"),
        name="peer_topk",
    )(q, keys1, keys2)


def _peer_gate_body(x_ref, u_ref, n1_ref, c1_ref, rk_ref, e2_ref, w_ref, *, te):
    j = pl.program_id(1)
    ht = lax.dot_general(u_ref[...].astype(BF16), x_ref[...], NT_DIMS, preferred_element_type=F32)
    for s in range(te // PEER_KEYS):
        i1 = j * (te // PEER_KEYS) + s
        gate = None
        for h in range(PEER_HEADS):
            count = n1_ref[h, pl.ds(i1, 1), :].astype(BF16)
            factor = c1_ref[h, pl.ds(i1, 1), :].astype(BF16)
            term = jnp.where(rk_ref[h] < count, e2_ref[h], jnp.zeros((), BF16)) * factor
            gate = term if gate is None else gate + term
        act = jax.nn.gelu(ht[s * PEER_KEYS:(s + 1) * PEER_KEYS], approximate=True).astype(BF16)
        w_ref[:, s * PEER_KEYS:(s + 1) * PEER_KEYS] = jnp.transpose(gate * act)


def peer_gate_call(hn, u, layer, n1, c1, rk, e2, tm=1024, te=256):
    t, d = hn.shape
    e = u.shape[1]
    tm = min(tm, t)
    hk = pl.BlockSpec((PEER_HEADS, PEER_KEYS, tm), lambda i, j: (0, 0, i))
    return pl.pallas_call(
        functools.partial(_peer_gate_body, te=te),
        grid=(t // tm, e // te),
        in_specs=[pl.BlockSpec((tm, d), lambda i, j: (i, 0)),
                  pl.BlockSpec((None, te, d), lambda i, j: (layer, j, 0)),
                  hk, hk, hk, hk],
        out_specs=pl.BlockSpec((tm, te), lambda i, j: (i, j)),
        out_shape=jax.ShapeDtypeStruct((t, e), BF16),
        compiler_params=_params("parallel", "arbitrary"),
        name="peer_gate",
    )(hn, u, n1, c1, rk, e2)


def _mm_acc_body(a_ref, w_ref, o_ref):
    @pl.when(pl.program_id(2) == 0)
    def _():
        o_ref[...] = jnp.zeros_like(o_ref)

    o_ref[...] += jnp.dot(a_ref[...], w_ref[...].astype(BF16), preferred_element_type=F32)


def mm_acc_call(a, w, layer, tm=2048, tn=1024, tk=1024, name="mm_acc"):
    m, k = a.shape
    n = w.shape[2]
    tm, tn, tk = min(tm, m), min(tn, n), min(tk, k)
    return pl.pallas_call(
        _mm_acc_body,
        grid=(m // tm, n // tn, k // tk),
        in_specs=[pl.BlockSpec((tm, tk), lambda i, j, l: (i, l)),
                  pl.BlockSpec((None, tk, tn), lambda i, j, l: (layer, l, j))],
        out_specs=pl.BlockSpec((tm, tn), lambda i, j, l: (i, j)),
        out_shape=jax.ShapeDtypeStruct((m, n), F32),
        compiler_params=_params("parallel", "parallel", "arbitrary"),
        name=name,
    )(a, w)


def peer_ffn_call(h, norm_g, w_q, layer, keys1, keys2, u, v):
    hn = rmsnorm_call(h, norm_g, BF16)
    q = mm_call(hn, w_q, layer, 0, w_q.shape[-1], BF16, name="peer_q")
    n1, c1, rk, e2 = peer_topk_call(q, keys1.astype(BF16), keys2.astype(BF16))
    w = peer_gate_call(hn, u, layer, n1, c1, rk, e2)
    return mm_acc_call(w, v, layer, name="peer_out")


def mixer_call(xn, h, w_in, w_out, layer, lb, hgrn_norm, s5_tables, ssm_d, ssm_w_glu, ssm_norm, rel_bias, att_norm,
               bsz):
    n_a = 4 * D_HGRN
    proj_a = mm_call(xn, w_in, layer, 0, n_a, F32, name="in_proj_a")
    proj_b = mm_call(xn, w_in, layer, n_a, D_SSM, F32, name="in_proj_b")
    proj_c = mm_call(xn, w_in, layer, n_a + D_SSM, 3 * D_ATT, BF16, name="in_proj_c")
    y_a = hgrn_call(proj_a, lb, hgrn_norm, bsz)
    y_b = s5_call(proj_b, s5_tables, ssm_d, ssm_w_glu, ssm_norm, bsz)
    y_c = attention_call(proj_c, rel_bias, att_norm, bsz)
    mixed = jnp.concatenate([y_a, y_b, y_c], axis=-1)
    return mm_call(mixed, w_out, layer, 0, w_out.shape[-1], F32, res=h, name="out_proj")


def kernel(x, w_in, w_out, hgrn_lb, hgrn_norm, ssm_a_re, ssm_a_im, ssm_log_dt, ssm_b_re, ssm_b_im, ssm_c_re, ssm_c_im, ssm_d, ssm_w_glu, ssm_norm, att_rel_bias, att_norm, norm_mix, norm_ffn, peer_w_q, peer_keys1, peer_keys2, peer_u, peer_v, norm_final):
    bsz, seq, d = x.shape
    lb_all = jnp.cumsum(jax.nn.softmax(hgrn_lb.astype(F32), axis=0), axis=0)
    lb_all = lb_all - lb_all[0]
    h = x.reshape(bsz * seq, d)
    xn = rmsnorm_call(h, norm_mix[0], BF16)
    out = None
    for layer in range(DEPTH):
        tables = _s5_prep(ssm_a_re[layer], ssm_a_im[layer], ssm_log_dt[layer], ssm_b_re[layer],
                          ssm_b_im[layer], ssm_c_re[layer], ssm_c_im[layer])
        h = mixer_call(xn, h, w_in, w_out, layer, lb_all[layer], hgrn_norm[layer], tables,
                       ssm_d[layer], ssm_w_glu[layer], ssm_norm[layer], att_rel_bias[layer],
                       att_norm[layer], bsz)
        ffn = peer_ffn_call(h, norm_ffn[layer], peer_w_q, layer, peer_keys1[layer], peer_keys2[layer],
                            peer_u, peer_v)
        if layer + 1 < DEPTH:
            h, xn = add_rmsnorm_call(h, ffn, norm_mix[layer + 1], BF16, keep_sum=True)
        else:
            out = add_rmsnorm_call(h, ffn, norm_final, F32, keep_sum=False)
    return out.reshape(bsz, seq, d)
```

```python
import functools
import math

import jax
import jax.numpy as jnp
import numpy as np
from jax import lax
from jax.experimental import pallas as pl
from jax.experimental.pallas import tpu as pltpu

F32 = jnp.float32
BF16 = jnp.bfloat16

EPS = 1e-6
DEPTH = 2
CHUNK = 64
D_HGRN = 1024
HGRN_HEADS = 8
HGRN_HEAD_DIM = D_HGRN // HGRN_HEADS
HGRN_TILE = 128
D_SSM = 1024
SSM_GROUP = 16
SSM_GROUPS = D_SSM // SSM_GROUP
SSM_STATE = 64
SSM_CHUNK = 16
SSM_GB = 8
D_ATT = 2048
ATT_HEADS = 16
ATT_HEAD_DIM = D_ATT // ATT_HEADS
LEFT_CHUNKS = 8
MAX_REL = 256
ATT_TQ = 256
ATT_WIN = ATT_TQ + LEFT_CHUNKS * CHUNK
PEER_HEADS = 8
PEER_KEYS = 128
PEER_QDIM = 256
PEER_TOPK = 16

VMEM_LIMIT = 56 * 1024 * 1024

NT_DIMS = (((1,), (1,)), ((), ()))
TN_DIMS = (((0,), (0,)), ((), ()))


def _params(*sem):
    return pltpu.CompilerParams(dimension_semantics=sem, vmem_limit_bytes=VMEM_LIMIT)


def _rms(x, g):
    ms = jnp.mean(x * x, axis=-1, keepdims=True)
    return x * lax.rsqrt(ms + EPS) * g


def _rmsnorm_body(x_ref, g_ref, o_ref):
    o_ref[...] = _rms(x_ref[...], g_ref[...]).astype(o_ref.dtype)


def _add_rmsnorm_body(a_ref, b_ref, g_ref, *out_refs):
    s = a_ref[...] + b_ref[...]
    if len(out_refs) == 2:
        out_refs[0][...] = s
    out_refs[-1][...] = _rms(s, g_ref[...]).astype(out_refs[-1].dtype)


def rmsnorm_call(x, g, out_dtype, tm=256):
    t, d = x.shape
    row = pl.BlockSpec((tm, d), lambda i: (i, 0))
    return pl.pallas_call(
        _rmsnorm_body,
        grid=(t // tm,),
        in_specs=[row, pl.BlockSpec((1, d), lambda i: (0, 0))],
        out_specs=row,
        out_shape=jax.ShapeDtypeStruct((t, d), out_dtype),
        compiler_params=_params("parallel"),
        name="rmsnorm",
    )(x, g.reshape(1, d).astype(F32))


def add_rmsnorm_call(a, b, g, out_dtype, keep_sum, tm=256):
    t, d = a.shape
    row = pl.BlockSpec((tm, d), lambda i: (i, 0))
    out_specs = [row, row] if keep_sum else row
    normed = jax.ShapeDtypeStruct((t, d), out_dtype)
    out_shape = [jax.ShapeDtypeStruct((t, d), F32), normed] if keep_sum else normed
    return pl.pallas_call(
        _add_rmsnorm_body,
        grid=(t // tm,),
        in_specs=[row, row, pl.BlockSpec((1, d), lambda i: (0, 0))],
        out_specs=out_specs,
        out_shape=out_shape,
        compiler_params=_params("parallel"),
        name="add_rmsnorm",
    )(a, b, g.reshape(1, d).astype(F32))


def _mm_body(a_ref, w_ref, o_ref):
    w = w_ref[...].astype(BF16)
    o_ref[...] = jnp.dot(a_ref[...], w, preferred_element_type=F32).astype(o_ref.dtype)


def _mm_res_body(a_ref, w_ref, r_ref, o_ref):
    acc = jnp.dot(a_ref[...], w_ref[...].astype(BF16), preferred_element_type=F32)
    o_ref[...] = (r_ref[...] + acc).astype(o_ref.dtype)


def mm_call(a, w, layer, col0, n, out_dtype, res=None, tm=1024, tn=512, name="mm"):
    m, k = a.shape
    tm, tn = min(tm, m), min(tn, n)
    c0 = col0 // tn
    assert col0 % tn == 0 and n % tn == 0
    in_specs = [pl.BlockSpec((tm, k), lambda i, j: (i, 0)),
                pl.BlockSpec((None, k, tn), lambda i, j: (layer, 0, c0 + j))]
    args = [a, w]
    body = _mm_body
    if res is not None:
        in_specs.append(pl.BlockSpec((tm, tn), lambda i, j: (i, j)))
        args.append(res)
        body = _mm_res_body
    return pl.pallas_call(
        body,
        grid=(m // tm, n // tn),
        in_specs=in_specs,
        out_specs=pl.BlockSpec((tm, tn), lambda i, j: (i, j)),
        out_shape=jax.ShapeDtypeStruct((m, n), out_dtype),
        compiler_params=_params("parallel", "arbitrary"),
        name=name,
    )(*args)


def _hgrn_consts():
    c = HGRN_TILE
    r = np.arange(c)
    t, i = r[:, None], r[None, :]
    mats_q, mats_k, masks = [], [], []
    w = c // 2
    while w >= 1:
        seg = (r // (2 * w)) * (2 * w)
        second = (r % (2 * w)) >= w
        mats_q.append(second[:, None] & (i >= (seg + w)[:, None]) & (i <= t))
        mats_k.append(~second[:, None] & (i >= t + 1) & (i <= (seg + w - 1)[:, None]))
        masks.append((seg[:, None] == seg[None, :]) & second[:, None] & ~second[None, :])
        w //= 2
    masks.append(t == i)
    mats = np.concatenate([i <= t, i > t] + mats_q + mats_k, axis=0).astype(np.float32)
    return (jnp.asarray(np.concatenate([mats, mats], axis=1), BF16),
            jnp.asarray(np.stack(masks).astype(np.float32), F32), len(mats_q))


def _hgrn_body(q_ref, f_ref, i_ref, g_ref, am_ref, mask_ref, prm_ref, o_ref,
               st_ref, qt_ref, kt_ref, ex_ref, *, levels):
    hd = HGRN_HEAD_DIM

    @pl.when(pl.program_id(1) == 0)
    def _():
        st_ref[...] = jnp.zeros_like(st_ref)

    z = f_ref[...]
    log_lb, log_1m_lb, one_m_lb, norm_g = prm_ref[0:1], prm_ref[1:2], prm_ref[2:3], prm_ref[3:4]
    e = jnp.exp(-jnp.abs(z))
    log_sig = jnp.minimum(z, 0.0) - jnp.log1p(e)
    cc = log_1m_lb + log_sig
    log_f = jnp.maximum(log_lb, cc) + jnp.log1p(jnp.exp(-jnp.abs(log_lb - cc)))
    k = one_m_lb * jnp.where(z >= 0.0, e, 1.0) / (1.0 + e)
    q = q_ref[...]
    lf_hi = log_f.astype(BF16)
    lf_lo = (log_f - lf_hi.astype(F32)).astype(BF16)
    tile = HGRN_TILE
    ex_ref[...] = jnp.dot(am_ref[...], jnp.concatenate([lf_hi, lf_lo], axis=0), preferred_element_type=F32)

    def decay(m):
        return jnp.exp(ex_ref[m * tile:(m + 1) * tile, :])

    eb = decay(0)
    decay_all = eb[tile - 1:tile, :]
    qt_ref[0] = (q * eb).astype(BF16)
    kt_ref[0] = (k * decay(1)).astype(BF16)
    for lv in range(levels):
        qt_ref[1 + lv] = (q * decay(2 + lv)).astype(BF16)
        kt_ref[1 + lv] = (k * decay(2 + levels + lv)).astype(BF16)
    qt_ref[1 + levels] = q.astype(BF16)
    kt_ref[1 + levels] = k.astype(BF16)

    for h in range(HGRN_HEADS):
        hs = slice(h * hd, (h + 1) * hd)
        scores = None
        for lv in range(levels + 1):
            sc = lax.dot_general(qt_ref[1 + lv, :, hs], kt_ref[1 + lv, :, hs], NT_DIMS,
                                 preferred_element_type=F32) * mask_ref[lv]
            scores = sc if scores is None else scores + sc
        v = i_ref[:, hs].astype(BF16)
        st = st_ref[h]
        o = jnp.dot(scores.astype(BF16), v, preferred_element_type=F32)
        o = o + lax.dot_general(qt_ref[0, :, hs], st.astype(BF16), NT_DIMS, preferred_element_type=F32)
        st_ref[h] = st * decay_all[:, hs] + lax.dot_general(v, kt_ref[0, :, hs], TN_DIMS,
                                                            preferred_element_type=F32)
        o = o * lax.rsqrt(jnp.mean(o * o, axis=-1, keepdims=True) + EPS) * norm_g[:, hs]
        o_ref[:, hs] = (o * jax.nn.silu(g_ref[:, hs])).astype(o_ref.dtype)


def hgrn_call(proj, lb, norm_g, bsz):
    t = proj.shape[0]
    nblk = t // bsz // HGRN_TILE
    am, masks, levels = _hgrn_consts()
    lb = lb.astype(F32)
    prm = jnp.stack([jnp.log(lb), jnp.log1p(-lb), 1.0 - lb, norm_g.astype(F32)])
    col = lambda c: pl.BlockSpec((HGRN_TILE, D_HGRN), lambda b, n: (b * nblk + n, c))
    whole = lambda a: pl.BlockSpec(a.shape, lambda b, n: (0,) * a.ndim)
    return pl.pallas_call(
        functools.partial(_hgrn_body, levels=levels),
        grid=(bsz, nblk),
        in_specs=[col(0), col(1), col(2), col(3), whole(am), whole(masks), whole(prm)],
        out_specs=pl.BlockSpec((HGRN_TILE, D_HGRN), lambda b, n: (b * nblk + n, 0)),
        out_shape=jax.ShapeDtypeStruct((t, D_HGRN), BF16),
        scratch_shapes=[pltpu.VMEM((HGRN_HEADS, HGRN_HEAD_DIM, HGRN_HEAD_DIM), F32),
                        pltpu.VMEM((levels + 2, HGRN_TILE, D_HGRN), BF16),
                        pltpu.VMEM((levels + 2, HGRN_TILE, D_HGRN), BF16),
                        pltpu.VMEM(((2 * levels + 2) * HGRN_TILE, D_HGRN), F32)],
        compiler_params=_params("parallel", "arbitrary"),
        name="hgrn2",
    )(proj, proj, proj, proj, am, masks, prm)


def _s5_prep(a_re, a_im, log_dt, b_re, b_im, c_re, c_im):
    ln, gb, hp = SSM_CHUNK, SSM_GB, lax.Precision.HIGHEST
    a = lax.complex(a_re.astype(F32), a_im.astype(F32))
    adt = a * jnp.exp(log_dt.astype(F32))[:, None]
    a_bar = jnp.exp(adt)
    b_bar = ((a_bar - 1.0) / a)[..., None] * lax.complex(b_re.astype(F32), b_im.astype(F32))
    c_mat = lax.complex(c_re.astype(F32), c_im.astype(F32))
    steps = jnp.arange(ln + 1, dtype=F32)
    pw = jnp.exp(adt[:, None, :] * steps[None, :, None])
    g, p, m = b_bar.shape
    nb = g // gb
    kern = jnp.einsum('gmp,gtp,gpn->gtmn', c_mat, pw[:, :ln], b_bar, precision=hp).real
    same = jnp.asarray(np.eye(gb, dtype=bool))[None, None, :, None, :, None]
    lag_t = kern.reshape(nb, gb, ln, m, m).transpose(0, 2, 1, 4, 3)
    lag_blocks = jnp.where(same, lag_t[:, :, :, :, None, :], 0).astype(BF16).reshape(nb, ln, gb * m, gb * m)
    back = jnp.exp(adt[:, None, :] * (ln - 1 - steps[:ln])[None, :, None])
    inc = back[:, :, None, :] * b_bar.transpose(0, 2, 1)[:, None, :, :]
    inc = jnp.concatenate([inc.real, inc.imag], axis=-1)
    inc = inc.reshape(nb, gb, ln, m, 2 * p).transpose(0, 2, 1, 3, 4).reshape(nb, ln * gb * m, 2 * p)
    out = c_mat.transpose(0, 2, 1)[:, :, None, :] * pw[:, 1:ln + 1].transpose(0, 2, 1)[:, :, :, None]
    out = jnp.concatenate([out.real, -out.imag], axis=1).reshape(nb, gb * 2 * p, ln * m)
    cp = jnp.exp(adt[:, None, :] * (ln * jnp.arange(9, dtype=F32))[None, :, None])
    rr = jnp.concatenate([cp.real, cp.real], axis=-1)
    ii = jnp.concatenate([-cp.imag, cp.imag], axis=-1)
    lanes = lambda x: x.reshape(nb, gb, 9, 2 * p).transpose(0, 2, 1, 3).reshape(nb, 9, gb * 2 * p)
    rr, ii = lanes(rr), lanes(ii)
    sel = np.array([1, 2, 4, 8])
    col = np.arange(ln * gb * m)
    expand = (np.arange(ln * m)[:, None] == ((col // (gb * m)) * m + col % m)[None, :]).astype(np.float32)
    return (lag_blocks, inc.astype(BF16), out.astype(BF16), jnp.asarray(expand, BF16),
            rr[:, :8], ii[:, :8], rr[:, sel], ii[:, sel])


def _s5_body(u_ref, lag_ref, inc_ref, out_ref, exp_ref, prr_ref, pii_ref, srr_ref, sii_ref, y_ref,
             d_ref, x_ref, bt_ref, mb_ref, mc_ref):
    ln = SSM_CHUNK
    lanes = SSM_GB * SSM_GROUP
    width = SSM_GB * 2 * SSM_STATE

    @pl.when(pl.program_id(1) == 0)
    def _():
        bt_ref[...] = jnp.zeros_like(bt_ref)
        for lp in range(ln):
            for l in range(lp, ln):
                bt_ref[lp * lanes:(lp + 1) * lanes, l * lanes:(l + 1) * lanes] = lag_ref[0, l - lp]
        r = lax.broadcasted_iota(jnp.int32, mb_ref.shape, 0)
        c = lax.broadcasted_iota(jnp.int32, mb_ref.shape, 1)
        own = ((r // SSM_GROUP) % SSM_GB) == (c // (2 * SSM_STATE))
        mb_ref[...] = jnp.where(own, jnp.concatenate([inc_ref[0]] * SSM_GB, axis=1), jnp.zeros((), BF16))
        r = lax.broadcasted_iota(jnp.int32, mc_ref.shape, 0)
        c = lax.broadcasted_iota(jnp.int32, mc_ref.shape, 1)
        own = (r // (2 * SSM_STATE)) == ((c % lanes) // SSM_GROUP)
        spread = jnp.dot(out_ref[0], exp_ref[...], preferred_element_type=F32).astype(BF16)
        mc_ref[...] = jnp.where(own, spread, jnp.zeros((), BF16))

    u = jnp.concatenate([u_ref[:, l, :].astype(BF16) for l in range(ln)], axis=1)
    d_ref[...] = jnp.dot(u, mb_ref[...], preferred_element_type=F32)
    half = SSM_STATE
    lane = lax.broadcasted_iota(jnp.int32, (8, width), 1)
    low = (lane % (2 * half)) < half

    def cmul(rr, ii, zz):
        swapped = jnp.where(low, pltpu.roll(zz, width - half, axis=1), pltpu.roll(zz, half, axis=1))
        return rr * zz + ii * swapped

    row = lax.broadcasted_iota(jnp.int32, (8, width), 0)
    prr, pii, srr, sii = prr_ref[0], pii_ref[0], srr_ref[0], sii_ref[0]
    carry = jnp.zeros((8, width), F32)
    for tl in range(d_ref.shape[0] // 8):
        r0 = tl * 8
        pre = d_ref[r0:r0 + 8, :]
        for n, s in enumerate((1, 2, 4)):
            shifted = jnp.where(row >= s, pltpu.roll(pre, s, axis=0), 0.0)
            pre = pre + cmul(srr[n:n + 1], sii[n:n + 1], shifted)
        x_ref[r0:r0 + 8, :] = cmul(prr, pii, carry) + jnp.where(row >= 1, pltpu.roll(pre, 1, axis=0), 0.0)
        carry = cmul(srr[3:4], sii[3:4], carry) + jnp.broadcast_to(pre[7:8, :], carry.shape)
    y = jnp.dot(u, bt_ref[...], preferred_element_type=F32)
    y = y + jnp.dot(x_ref[...].astype(BF16), mc_ref[...], preferred_element_type=F32)
    for l in range(ln):
        y_ref[:, l, :] = y[:, l * lanes:(l + 1) * lanes]


def _s5_post_body(y_ref, u_ref, prm_ref, w_ref, o_ref):
    y = y_ref[...] + prm_ref[0:1] * u_ref[...]
    y = jax.nn.gelu(y, approximate=True)
    gate = jnp.dot(y.astype(BF16), w_ref[...], preferred_element_type=F32)
    o_ref[...] = _rms(y * jax.nn.sigmoid(gate), prm_ref[1:2]).astype(o_ref.dtype)


def s5_call(u, tables, d_skip, w_glu, norm_g, bsz, tm=512):
    t = u.shape[0]
    rows = t // SSM_CHUNK
    lanes = SSM_GB * SSM_GROUP
    blk = pl.BlockSpec((rows // bsz, SSM_CHUNK, lanes), lambda g, b: (b, 0, g))
    tab = lambda a: (pl.BlockSpec((1,) + a.shape[1:], lambda g, b: (g,) + (0,) * (a.ndim - 1)) if a.ndim > 2
                     else pl.BlockSpec(a.shape, lambda g, b: (0, 0)))
    width = SSM_GB * 2 * SSM_STATE
    state = pltpu.VMEM((rows // bsz, width), F32)
    y = pl.pallas_call(
        _s5_body,
        grid=(SSM_GROUPS // SSM_GB, bsz),
        in_specs=[blk] + [tab(a) for a in tables],
        out_specs=blk,
        out_shape=jax.ShapeDtypeStruct((rows, SSM_CHUNK, D_SSM), F32),
        scratch_shapes=[state, state, pltpu.VMEM((SSM_CHUNK * lanes, SSM_CHUNK * lanes), BF16),
                        pltpu.VMEM((SSM_CHUNK * lanes, width), BF16), pltpu.VMEM((width, SSM_CHUNK * lanes), BF16)],
        compiler_params=_params("parallel", "arbitrary"),
        name="s5_scan",
    )(u.reshape(rows, SSM_CHUNK, D_SSM), *tables)
    tm = min(tm, t)
    prm = jnp.stack([d_skip.astype(F32), norm_g.astype(F32)])
    row = pl.BlockSpec((tm, D_SSM), lambda i: (i, 0))
    return pl.pallas_call(
        _s5_post_body,
        grid=(t // tm,),
        in_specs=[row, row, pl.BlockSpec((2, D_SSM), lambda i: (0, 0)),
                  pl.BlockSpec((D_SSM, D_SSM), lambda i: (0, 0))],
        out_specs=row,
        out_shape=jax.ShapeDtypeStruct((t, D_SSM), BF16),
        compiler_params=_params("parallel"),
        name="s5_post",
    )(y.reshape(t, D_SSM), u, prm, w_glu.astype(BF16))


def _att_tables(rel_bias):
    qi = np.arange(ATT_TQ)[:, None]
    kj = np.arange(ATT_WIN)[None, :]
    back = kj // CHUNK - qi // CHUNK
    valid = (back >= 0) & (back <= LEFT_CHUNKS)
    rb = rel_bias.astype(F32)
    nh = rb.shape[0]
    shift = LEFT_CHUNKS * CHUNK
    n_mid = ATT_WIN - (shift - MAX_REL)
    assert shift >= MAX_REL and n_mid <= 2 * MAX_REL + 1
    desc = rb[:, ::-1]
    far = lambda n: jnp.broadcast_to(desc[:, :1], (nh, n))
    g = jnp.concatenate([far(shift - MAX_REL), desc[:, :n_mid], far(ATT_TQ)], axis=1)
    return g[:, None, :], jnp.asarray(np.where(valid, 0.0, -1e30), F32)


def _att_body(q_ref, k_ref, v_ref, g_ref, m_ref, o_ref):
    scale = ATT_HEAD_DIM ** -0.5
    lead = ATT_WIN // ATT_TQ - 1
    rows = jnp.broadcast_to(g_ref[0], (ATT_TQ, g_ref.shape[2]))
    bias_full = pltpu.roll(rows, 0, 1, stride=1, stride_axis=0)[:, :ATT_WIN] + m_ref[...]
    for i in range(q_ref.shape[1] // ATT_TQ):
        q0 = i * ATT_TQ
        k0 = max(0, q0 - (ATT_WIN - ATT_TQ))
        nk = q0 + ATT_TQ - k0
        bias = bias_full if i >= lead else bias_full[:, ATT_WIN - nk:]
        s = lax.dot_general(q_ref[0, q0:q0 + ATT_TQ, :], k_ref[0, k0:k0 + nk, :], NT_DIMS,
                            preferred_element_type=F32) * scale + bias
        p = jnp.exp(s - jnp.max(s, axis=-1, keepdims=True))
        l = jnp.sum(p, axis=-1, keepdims=True)
        o_ref[0, q0:q0 + ATT_TQ, :] = jnp.dot(p.astype(BF16), v_ref[0, k0:k0 + nk, :],
                                              preferred_element_type=F32) / l


def attention_call(qkv, rel_bias, norm_g, bsz):
    t = qkv.shape[0]
    seq = t // bsz
    x = qkv.reshape(bsz, seq, 3 * D_ATT)
    g, mask = _att_tables(rel_bias)
    head = lambda c: pl.BlockSpec((1, seq, ATT_HEAD_DIM), lambda b, h: (b, 0, c * ATT_HEADS + h))
    o = pl.pallas_call(
        _att_body,
        grid=(bsz, ATT_HEADS),
        in_specs=[head(0), head(1), head(2), pl.BlockSpec((1, 1, g.shape[2]), lambda b, h: (h, 0, 0)),
                  pl.BlockSpec(mask.shape, lambda b, h: (0, 0))],
        out_specs=head(0),
        out_shape=jax.ShapeDtypeStruct((bsz, seq, D_ATT), F32),
        compiler_params=_params("parallel", "parallel"),
        name="chunk_attention",
    )(x, x, x, g, mask)
    return rmsnorm_call(o.reshape(t, D_ATT), norm_g, BF16)


def _top_rows(x, k):
    vals, where_ = [], []
    idx = lax.broadcasted_iota(jnp.int32, x.shape, 0).astype(F32)
    for _ in range(k):
        m = jnp.max(x, axis=0, keepdims=True)
        first = jnp.min(jnp.where(x == m, idx, float(x.shape[0])), axis=0, keepdims=True)
        x = jnp.where(idx == first, -jnp.inf, x)
        vals.append(m)
        where_.append(first)
    return vals, where_


def _peer_topk_body(q_ref, k1_ref, k2_ref, n1_ref, c1_ref, rk_ref, e2_ref):
    half = PEER_QDIM // 2
    k = PEER_TOPK
    for h in range(PEER_HEADS):
        base = h * PEER_QDIM
        s1 = lax.dot_general(k1_ref[...], q_ref[:, base:base + half], NT_DIMS, preferred_element_type=F32)
        s2 = lax.dot_general(k2_ref[...], q_ref[:, base + half:base + PEER_QDIM], NT_DIMS,
                             preferred_element_type=F32)
        v1, at1 = _top_rows(s1, k)
        v2, at2 = _top_rows(s2, k)
        v2 = jnp.concatenate(v2, axis=0)
        cand = jnp.concatenate([v1[a] + v2[:k // (a + 1)] for a in range(k)], axis=0)
        top, _ = _top_rows(cand, k)
        tau = top[k - 1]
        z = jnp.ones_like(tau)
        for t in top[1:]:
            z = z + jnp.exp(t - top[0])
        idx = lax.broadcasted_iota(jnp.int32, s1.shape, 0).astype(F32)
        count = jnp.zeros_like(s1)
        rank = jnp.full_like(s2, float(k))
        for a in range(k):
            n_a = jnp.sum(jnp.where(v1[a] + v2 >= tau, 1.0, 0.0), axis=0, keepdims=True)
            count = jnp.where(idx == at1[a], n_a, count)
            rank = jnp.where(idx == at2[a], float(a), rank)
        n1_ref[h] = count
        c1_ref[h] = jnp.exp(s1 - v1[0]) / z
        rk_ref[h] = rank.astype(rk_ref.dtype)
        e2_ref[h] = jnp.exp(s2 - v2[0:1]).astype(e2_ref.dtype)


def peer_topk_call(q, keys1, keys2, tm=256):
    t = q.shape[0]
    hk = (PEER_HEADS, PEER_KEYS, t)
    blk = pl.BlockSpec((PEER_HEADS, PEER_KEYS, tm), lambda i: (0, 0, i))
    return pl.pallas_call(
        _peer_topk_body,
        grid=(t // tm,),
        in_specs=[pl.BlockSpec((tm, PEER_HEADS * PEER_QDIM), lambda i: (i, 0)),
                  pl.BlockSpec((PEER_KEYS, PEER_QDIM // 2), lambda i: (0, 0)),
                  pl.BlockSpec((PEER_KEYS, PEER_QDIM // 2), lambda i: (0, 0))],
        out_specs=[blk, blk, blk, blk],
        out_shape=[jax.ShapeDtypeStruct(hk, F32), jax.ShapeDtypeStruct(hk, F32),
                   jax.ShapeDtypeStruct(hk, BF16), jax.ShapeDtypeStruct(hk, BF16)],
        compiler_params=_params("parallel"),
        name="peer_topk",
    )(q, keys1, keys2)


def _peer_gate_body(x_ref, u_ref, n1_ref, c1_ref, rk_ref, e2_ref, w_ref, *, te):
    j = pl.program_id(1)
    ht = lax.dot_general(u_ref[...].astype(BF16), x_ref[...], NT_DIMS, preferred_element_type=F32)
    for s in range(te // PEER_KEYS):
        i1 = j * (te // PEER_KEYS) + s
        gate = None
        for h in range(PEER_HEADS):
            count = n1_ref[h, pl.ds(i1, 1), :].astype(BF16)
            factor = c1_ref[h, pl.ds(i1, 1), :].astype(BF16)
            term = jnp.where(rk_ref[h] < count, e2_ref[h], jnp.zeros((), BF16)) * factor
            gate = term if gate is None else gate + term
        act = jax.nn.gelu(ht[s * PEER_KEYS:(s + 1) * PEER_KEYS], approximate=True).astype(BF16)
        w_ref[:, s * PEER_KEYS:(s + 1) * PEER_KEYS] = jnp.transpose(gate * act)


def peer_gate_call(hn, u, layer, n1, c1, rk, e2, tm=1024, te=256):
    t, d = hn.shape
    e = u.shape[1]
    tm = min(tm, t)
    hk = pl.BlockSpec((PEER_HEADS, PEER_KEYS, tm), lambda i, j: (0, 0, i))
    return pl.pallas_call(
        functools.partial(_peer_gate_body, te=te),
        grid=(t // tm, e // te),
        in_specs=[pl.BlockSpec((tm, d), lambda i, j: (i, 0)),
                  pl.BlockSpec((None, te, d), lambda i, j: (layer, j, 0)),
                  hk, hk, hk, hk],
        out_specs=pl.BlockSpec((tm, te), lambda i, j: (i, j)),
        out_shape=jax.ShapeDtypeStruct((t, e), BF16),
        compiler_params=_params("parallel", "arbitrary"),
        name="peer_gate",
    )(hn, u, n1, c1, rk, e2)


def _mm_acc_body(a_ref, w_ref, o_ref):
    @pl.when(pl.program_id(2) == 0)
    def _():
        o_ref[...] = jnp.zeros_like(o_ref)

    o_ref[...] += jnp.dot(a_ref[...], w_ref[...].astype(BF16), preferred_element_type=F32)


def mm_acc_call(a, w, layer, tm=2048, tn=1024, tk=1024, name="mm_acc"):
    m, k = a.shape
    n = w.shape[2]
    tm, tn, tk = min(tm, m), min(tn, n), min(tk, k)
    return pl.pallas_call(
        _mm_acc_body,
        grid=(m // tm, n // tn, k // tk),
        in_specs=[pl.BlockSpec((tm, tk), lambda i, j, l: (i, l)),
                  pl.BlockSpec((None, tk, tn), lambda i, j, l: (layer, l, j))],
        out_specs=pl.BlockSpec((tm, tn), lambda i, j, l: (i, j)),
        out_shape=jax.ShapeDtypeStruct((m, n), F32),
        compiler_params=_params("parallel", "parallel", "arbitrary"),
        name=name,
    )(a, w)


def peer_ffn_call(h, norm_g, w_q, layer, keys1, keys2, u, v):
    hn = rmsnorm_call(h, norm_g, BF16)
    q = mm_call(hn, w_q, layer, 0, w_q.shape[-1], BF16, name="peer_q")
    n1, c1, rk, e2 = peer_topk_call(q, keys1.astype(BF16), keys2.astype(BF16))
    w = peer_gate_call(hn, u, layer, n1, c1, rk, e2)
    return mm_acc_call(w, v, layer, name="peer_out")


def mixer_call(xn, h, w_in, w_out, layer, lb, hgrn_norm, s5_tables, ssm_d, ssm_w_glu, ssm_norm, rel_bias, att_norm,
               bsz):
    n_a = 4 * D_HGRN
    proj_a = mm_call(xn, w_in, layer, 0, n_a, F32, name="in_proj_a")
    proj_b = mm_call(xn, w_in, layer, n_a, D_SSM, F32, name="in_proj_b")
    proj_c = mm_call(xn, w_in, layer, n_a + D_SSM, 3 * D_ATT, BF16, name="in_proj_c")
    y_a = hgrn_call(proj_a, lb, hgrn_norm, bsz)
    y_b = s5_call(proj_b, s5_tables, ssm_d, ssm_w_glu, ssm_norm, bsz)
    y_c = attention_call(proj_c, rel_bias, att_norm, bsz)
    mixed = jnp.concatenate([y_a, y_b, y_c], axis=-1)
    return mm_call(mixed, w_out, layer, 0, w_out.shape[-1], F32, res=h, name="out_proj")


def kernel(x, w_in, w_out, hgrn_lb, hgrn_norm, ssm_a_re, ssm_a_im, ssm_log_dt, ssm_b_re, ssm_b_im, ssm_c_re, ssm_c_im, ssm_d, ssm_w_glu, ssm_norm, att_rel_bias, att_norm, norm_mix, norm_ffn, peer_w_q, peer_keys1, peer_keys2, peer_u, peer_v, norm_final):
    bsz, seq, d = x.shape
    lb_all = jnp.cumsum(jax.nn.softmax(hgrn_lb.astype(F32), axis=0), axis=0)
    lb_all = lb_all - lb_all[0]
    h = x.reshape(bsz * seq, d)
    xn = rmsnorm_call(h, norm_mix[0], BF16)
    out = None
    for layer in range(DEPTH):
        tables = _s5_prep(ssm_a_re[layer], ssm_a_im[layer], ssm_log_dt[layer], ssm_b_re[layer],
                          ssm_b_im[layer], ssm_c_re[layer], ssm_c_im[layer])
        h = mixer_call(xn, h, w_in, w_out, layer, lb_all[layer], hgrn_norm[layer], tables,
                       ssm_d[layer], ssm_w_glu[layer], ssm_norm[layer], att_rel_bias[layer],
                       att_norm[layer], bsz)
        ffn = peer_ffn_call(h, norm_ffn[layer], peer_w_q, layer, peer_keys1[layer], peer_keys2[layer],
                            peer_u, peer_v)
        if layer + 1 < DEPTH:
            h, xn = add_rmsnorm_call(h, ffn, norm_mix[layer + 1], BF16, keep_sum=True)
        else:
            out = add_rmsnorm_call(h, ffn, norm_final, F32, keep_sum=False)
    return out.reshape(bsz, seq, d)
```

```python
import functools
import math

import jax
import jax.numpy as jnp
import numpy as np
from jax import lax
from jax.experimental import pallas as pl
from jax.experimental.pallas import tpu as pltpu

F32 = jnp.float32
BF16 = jnp.bfloat16

EPS = 1e-6
DEPTH = 2
CHUNK = 64
D_HGRN = 1024
HGRN_HEADS = 8
HGRN_HEAD_DIM = D_HGRN // HGRN_HEADS
HGRN_TILE = 128
D_SSM = 1024
SSM_GROUP = 16
SSM_GROUPS = D_SSM // SSM_GROUP
SSM_STATE = 64
SSM_CHUNK = 16
SSM_GB = 8
D_ATT = 2048
ATT_HEADS = 16
ATT_HEAD_DIM = D_ATT // ATT_HEADS
LEFT_CHUNKS = 8
MAX_REL = 256
ATT_TQ = 256
ATT_WIN = ATT_TQ + LEFT_CHUNKS * CHUNK
PEER_HEADS = 8
PEER_KEYS = 128
PEER_QDIM = 256
PEER_TOPK = 16

VMEM_LIMIT = 56 * 1024 * 1024

NT_DIMS = (((1,), (1,)), ((), ()))
TN_DIMS = (((0,), (0,)), ((), ()))


def _params(*sem):
    return pltpu.CompilerParams(dimension_semantics=sem, vmem_limit_bytes=VMEM_LIMIT)


def _rms(x, g):
    ms = jnp.mean(x * x, axis=-1, keepdims=True)
    return x * lax.rsqrt(ms + EPS) * g


def _rmsnorm_body(x_ref, g_ref, o_ref):
    o_ref[...] = _rms(x_ref[...], g_ref[...]).astype(o_ref.dtype)


def _add_rmsnorm_body(a_ref, b_ref, g_ref, *out_refs):
    s = a_ref[...] + b_ref[...]
    if len(out_refs) == 2:
        out_refs[0][...] = s
    out_refs[-1][...] = _rms(s, g_ref[...]).astype(out_refs[-1].dtype)


def rmsnorm_call(x, g, out_dtype, tm=256):
    t, d = x.shape
    row = pl.BlockSpec((tm, d), lambda i: (i, 0))
    return pl.pallas_call(
        _rmsnorm_body,
        grid=(t // tm,),
        in_specs=[row, pl.BlockSpec((1, d), lambda i: (0, 0))],
        out_specs=row,
        out_shape=jax.ShapeDtypeStruct((t, d), out_dtype),
        compiler_params=_params("parallel"),
        name="rmsnorm",
    )(x, g.reshape(1, d).astype(F32))


def add_rmsnorm_call(a, b, g, out_dtype, keep_sum, tm=256):
    t, d = a.shape
    row = pl.BlockSpec((tm, d), lambda i: (i, 0))
    out_specs = [row, row] if keep_sum else row
    normed = jax.ShapeDtypeStruct((t, d), out_dtype)
    out_shape = [jax.ShapeDtypeStruct((t, d), F32), normed] if keep_sum else normed
    return pl.pallas_call(
        _add_rmsnorm_body,
        grid=(t // tm,),
        in_specs=[row, row, pl.BlockSpec((1, d), lambda i: (0, 0))],
        out_specs=out_specs,
        out_shape=out_shape,
        compiler_params=_params("parallel"),
        name="add_rmsnorm",
    )(a, b, g.reshape(1, d).astype(F32))


def _mm_body(a_ref, w_ref, o_ref):
    w = w_ref[...].astype(BF16)
    o_ref[...] = jnp.dot(a_ref[...], w, preferred_element_type=F32).astype(o_ref.dtype)


def _mm_res_body(a_ref, w_ref, r_ref, o_ref):
    acc = jnp.dot(a_ref[...], w_ref[...].astype(BF16), preferred_element_type=F32)
    o_ref[...] = (r_ref[...] + acc).astype(o_ref.dtype)


def _mm_parts_res_body(*refs):
    *a_refs, w_ref, r_ref, o_ref = refs
    acc, k0 = r_ref[...], 0
    for a_ref in a_refs:
        k1 = k0 + a_ref.shape[1]
        acc = acc + jnp.dot(a_ref[...], w_ref[k0:k1, :].astype(BF16), preferred_element_type=F32)
        k0 = k1
    o_ref[...] = acc.astype(o_ref.dtype)


def mm_parts_call(parts, w, layer, res, tm=1024, tn=512, name="mm_parts"):
    m = parts[0].shape[0]
    k, n = w.shape[1], w.shape[2]
    assert sum(p.shape[1] for p in parts) == k
    tm, tn = min(tm, m), min(tn, n)
    tile = pl.BlockSpec((tm, tn), lambda i, j: (i, j))
    return pl.pallas_call(
        _mm_parts_res_body,
        grid=(m // tm, n // tn),
        in_specs=[pl.BlockSpec((tm, p.shape[1]), lambda i, j: (i, 0)) for p in parts]
        + [pl.BlockSpec((None, k, tn), lambda i, j: (layer, 0, j)), tile],
        out_specs=tile,
        out_shape=jax.ShapeDtypeStruct((m, n), F32),
        compiler_params=_params("parallel", "arbitrary"),
        name=name,
    )(*parts, w, res)


def mm_call(a, w, layer, col0, n, out_dtype, res=None, tm=1024, tn=512, name="mm"):
    m, k = a.shape
    tm, tn = min(tm, m), min(tn, n)
    c0 = col0 // tn
    assert col0 % tn == 0 and n % tn == 0
    in_specs = [pl.BlockSpec((tm, k), lambda i, j: (i, 0)),
                pl.BlockSpec((None, k, tn), lambda i, j: (layer, 0, c0 + j))]
    args = [a, w]
    body = _mm_body
    if res is not None:
        in_specs.append(pl.BlockSpec((tm, tn), lambda i, j: (i, j)))
        args.append(res)
        body = _mm_res_body
    return pl.pallas_call(
        body,
        grid=(m // tm, n // tn),
        in_specs=in_specs,
        out_specs=pl.BlockSpec((tm, tn), lambda i, j: (i, j)),
        out_shape=jax.ShapeDtypeStruct((m, n), out_dtype),
        compiler_params=_params("parallel", "arbitrary"),
        name=name,
    )(*args)


def _hgrn_consts():
    c = HGRN_TILE
    r = np.arange(c)
    t, i = r[:, None], r[None, :]
    mats_q, mats_k, masks = [], [], []
    w = c // 2
    while w >= 1:
        seg = (r // (2 * w)) * (2 * w)
        second = (r % (2 * w)) >= w
        mats_q.append(second[:, None] & (i >= (seg + w)[:, None]) & (i <= t))
        mats_k.append(~second[:, None] & (i >= t + 1) & (i <= (seg + w - 1)[:, None]))
        masks.append((seg[:, None] == seg[None, :]) & second[:, None] & ~second[None, :])
        w //= 2
    masks.append(t == i)
    mats = np.concatenate([i <= t, i > t] + mats_q + mats_k, axis=0).astype(np.float32)
    return (jnp.asarray(np.concatenate([mats, mats], axis=1), BF16),
            jnp.asarray(np.stack(masks).astype(np.float32), F32), len(mats_q))


def _hgrn_body(q_ref, f_ref, i_ref, g_ref, am_ref, mask_ref, prm_ref, o_ref,
               st_ref, qt_ref, kt_ref, ex_ref, *, levels):
    hd = HGRN_HEAD_DIM

    @pl.when(pl.program_id(1) == 0)
    def _():
        st_ref[...] = jnp.zeros_like(st_ref)

    z = f_ref[...]
    log_lb, log_1m_lb, one_m_lb, norm_g = prm_ref[0:1], prm_ref[1:2], prm_ref[2:3], prm_ref[3:4]
    e = jnp.exp(-jnp.abs(z))
    log_sig = jnp.minimum(z, 0.0) - jnp.log1p(e)
    cc = log_1m_lb + log_sig
    log_f = jnp.maximum(log_lb, cc) + jnp.log1p(jnp.exp(-jnp.abs(log_lb - cc)))
    k = one_m_lb * jnp.where(z >= 0.0, e, 1.0) / (1.0 + e)
    q = q_ref[...]
    lf_hi = log_f.astype(BF16)
    lf_lo = (log_f - lf_hi.astype(F32)).astype(BF16)
    tile = HGRN_TILE
    ex_ref[...] = jnp.dot(am_ref[...], jnp.concatenate([lf_hi, lf_lo], axis=0), preferred_element_type=F32)

    def decay(m):
        return jnp.exp(ex_ref[m * tile:(m + 1) * tile, :])

    eb = decay(0)
    decay_all = eb[tile - 1:tile, :]
    qt_ref[0] = (q * eb).astype(BF16)
    kt_ref[0] = (k * decay(1)).astype(BF16)
    for lv in range(levels):
        qt_ref[1 + lv] = (q * decay(2 + lv)).astype(BF16)
        kt_ref[1 + lv] = (k * decay(2 + levels + lv)).astype(BF16)
    qt_ref[1 + levels] = q.astype(BF16)
    kt_ref[1 + levels] = k.astype(BF16)

    for h in range(HGRN_HEADS):
        hs = slice(h * hd, (h + 1) * hd)
        scores = None
        for lv in range(levels + 1):
            sc = lax.dot_general(qt_ref[1 + lv, :, hs], kt_ref[1 + lv, :, hs], NT_DIMS,
                                 preferred_element_type=F32) * mask_ref[lv]
            scores = sc if scores is None else scores + sc
        v = i_ref[:, hs].astype(BF16)
        st = st_ref[h]
        o = jnp.dot(scores.astype(BF16), v, preferred_element_type=F32)
        o = o + lax.dot_general(qt_ref[0, :, hs], st.astype(BF16), NT_DIMS, preferred_element_type=F32)
        st_ref[h] = st * decay_all[:, hs] + lax.dot_general(v, kt_ref[0, :, hs], TN_DIMS,
                                                            preferred_element_type=F32)
        o = o * lax.rsqrt(jnp.mean(o * o, axis=-1, keepdims=True) + EPS) * norm_g[:, hs]
        o_ref[:, hs] = (o * jax.nn.silu(g_ref[:, hs])).astype(o_ref.dtype)


def hgrn_call(proj, lb, norm_g, bsz):
    t = proj.shape[0]
    nblk = t // bsz // HGRN_TILE
    am, masks, levels = _hgrn_consts()
    lb = lb.astype(F32)
    prm = jnp.stack([jnp.log(lb), jnp.log1p(-lb), 1.0 - lb, norm_g.astype(F32)])
    col = lambda c: pl.BlockSpec((HGRN_TILE, D_HGRN), lambda b, n: (b * nblk + n, c))
    whole = lambda a: pl.BlockSpec(a.shape, lambda b, n: (0,) * a.ndim)
    return pl.pallas_call(
        functools.partial(_hgrn_body, levels=levels),
        grid=(bsz, nblk),
        in_specs=[col(0), col(1), col(2), col(3), whole(am), whole(masks), whole(prm)],
        out_specs=pl.BlockSpec((HGRN_TILE, D_HGRN), lambda b, n: (b * nblk + n, 0)),
        out_shape=jax.ShapeDtypeStruct((t, D_HGRN), BF16),
        scratch_shapes=[pltpu.VMEM((HGRN_HEADS, HGRN_HEAD_DIM, HGRN_HEAD_DIM), F32),
                        pltpu.VMEM((levels + 2, HGRN_TILE, D_HGRN), BF16),
                        pltpu.VMEM((levels + 2, HGRN_TILE, D_HGRN), BF16),
                        pltpu.VMEM(((2 * levels + 2) * HGRN_TILE, D_HGRN), F32)],
        compiler_params=_params("parallel", "arbitrary"),
        name="hgrn2",
    )(proj, proj, proj, proj, am, masks, prm)


def _s5_prep(a_re, a_im, log_dt, b_re, b_im, c_re, c_im):
    ln, gb, hp = SSM_CHUNK, SSM_GB, lax.Precision.HIGHEST
    a = lax.complex(a_re.astype(F32), a_im.astype(F32))
    adt = a * jnp.exp(log_dt.astype(F32))[:, None]
    a_bar = jnp.exp(adt)
    b_bar = ((a_bar - 1.0) / a)[..., None] * lax.complex(b_re.astype(F32), b_im.astype(F32))
    c_mat = lax.complex(c_re.astype(F32), c_im.astype(F32))
    steps = jnp.arange(ln + 1, dtype=F32)
    pw = jnp.exp(adt[:, None, :] * steps[None, :, None])
    g, p, m = b_bar.shape
    nb = g // gb
    kern = jnp.einsum('gmp,gtp,gpn->gtmn', c_mat, pw[:, :ln], b_bar, precision=hp).real
    same = jnp.asarray(np.eye(gb, dtype=bool))[None, None, :, None, :, None]
    lag_t = kern.reshape(nb, gb, ln, m, m).transpose(0, 2, 1, 4, 3)
    lag_blocks = jnp.where(same, lag_t[:, :, :, :, None, :], 0).astype(BF16).reshape(nb, ln, gb * m, gb * m)
    back = jnp.exp(adt[:, None, :] * (ln - 1 - steps[:ln])[None, :, None])
    inc = back[:, :, None, :] * b_bar.transpose(0, 2, 1)[:, None, :, :]
    inc = jnp.concatenate([inc.real, inc.imag], axis=-1)
    inc = inc.reshape(nb, gb, ln, m, 2 * p).transpose(0, 2, 1, 3, 4).reshape(nb, ln * gb * m, 2 * p)
    out = c_mat.transpose(0, 2, 1)[:, :, None, :] * pw[:, 1:ln + 1].transpose(0, 2, 1)[:, :, :, None]
    out = jnp.concatenate([out.real, -out.imag], axis=1).reshape(nb, gb * 2 * p, ln * m)
    cp = jnp.exp(adt[:, None, :] * (ln * jnp.arange(9, dtype=F32))[None, :, None])
    rr = jnp.concatenate([cp.real, cp.real], axis=-1)
    ii = jnp.concatenate([-cp.imag, cp.imag], axis=-1)
    lanes = lambda x: x.reshape(nb, gb, 9, 2 * p).transpose(0, 2, 1, 3).reshape(nb, 9, gb * 2 * p)
    rr, ii = lanes(rr), lanes(ii)
    sel = np.array([1, 2, 4, 8])
    col = np.arange(ln * gb * m)
    expand = (np.arange(ln * m)[:, None] == ((col // (gb * m)) * m + col % m)[None, :]).astype(np.float32)
    return (lag_blocks, inc.astype(BF16), out.astype(BF16), jnp.asarray(expand, BF16),
            rr[:, :8], ii[:, :8], rr[:, sel], ii[:, sel])


def _s5_body(u_ref, lag_ref, inc_ref, out_ref, exp_ref, prr_ref, pii_ref, srr_ref, sii_ref, y_ref,
             d_ref, x_ref, bt_ref, mb_ref, mc_ref):
    ln = SSM_CHUNK
    lanes = SSM_GB * SSM_GROUP
    width = SSM_GB * 2 * SSM_STATE

    @pl.when(pl.program_id(1) == 0)
    def _():
        bt_ref[...] = jnp.zeros_like(bt_ref)
        for lp in range(ln):
            for l in range(lp, ln):
                bt_ref[lp * lanes:(lp + 1) * lanes, l * lanes:(l + 1) * lanes] = lag_ref[0, l - lp]
        r = lax.broadcasted_iota(jnp.int32, mb_ref.shape, 0)
        c = lax.broadcasted_iota(jnp.int32, mb_ref.shape, 1)
        own = ((r // SSM_GROUP) % SSM_GB) == (c // (2 * SSM_STATE))
        mb_ref[...] = jnp.where(own, jnp.concatenate([inc_ref[0]] * SSM_GB, axis=1), jnp.zeros((), BF16))
        r = lax.broadcasted_iota(jnp.int32, mc_ref.shape, 0)
        c = lax.broadcasted_iota(jnp.int32, mc_ref.shape, 1)
        own = (r // (2 * SSM_STATE)) == ((c % lanes) // SSM_GROUP)
        spread = jnp.dot(out_ref[0], exp_ref[...], preferred_element_type=F32).astype(BF16)
        mc_ref[...] = jnp.where(own, spread, jnp.zeros((), BF16))

    u = jnp.concatenate([u_ref[:, l, :].astype(BF16) for l in range(ln)], axis=1)
    d_ref[...] = jnp.dot(u, mb_ref[...], preferred_element_type=F32)
    half = SSM_STATE
    lane = lax.broadcasted_iota(jnp.int32, (8, width), 1)
    low = (lane % (2 * half)) < half

    def cmul(rr, ii, zz):
        swapped = jnp.where(low, pltpu.roll(zz, width - half, axis=1), pltpu.roll(zz, half, axis=1))
        return rr * zz + ii * swapped

    row = lax.broadcasted_iota(jnp.int32, (8, width), 0)
    prr, pii, srr, sii = prr_ref[0], pii_ref[0], srr_ref[0], sii_ref[0]
    carry = jnp.zeros((8, width), F32)
    for tl in range(d_ref.shape[0] // 8):
        r0 = tl * 8
        pre = d_ref[r0:r0 + 8, :]
        for n, s in enumerate((1, 2, 4)):
            shifted = jnp.where(row >= s, pltpu.roll(pre, s, axis=0), 0.0)
            pre = pre + cmul(srr[n:n + 1], sii[n:n + 1], shifted)
        x_ref[r0:r0 + 8, :] = cmul(prr, pii, carry) + jnp.where(row >= 1, pltpu.roll(pre, 1, axis=0), 0.0)
        carry = cmul(srr[3:4], sii[3:4], carry) + jnp.broadcast_to(pre[7:8, :], carry.shape)
    y = jnp.dot(u, bt_ref[...], preferred_element_type=F32)
    y = y + jnp.dot(x_ref[...].astype(BF16), mc_ref[...], preferred_element_type=F32)
    for l in range(ln):
        y_ref[:, l, :] = y[:, l * lanes:(l + 1) * lanes]


def _s5_post_body(y_ref, u_ref, prm_ref, w_ref, o_ref):
    y = y_ref[...] + prm_ref[0:1] * u_ref[...]
    y = jax.nn.gelu(y, approximate=True)
    gate = jnp.dot(y.astype(BF16), w_ref[...], preferred_element_type=F32)
    o_ref[...] = _rms(y * jax.nn.sigmoid(gate), prm_ref[1:2]).astype(o_ref.dtype)


def s5_call(u, tables, d_skip, w_glu, norm_g, bsz, tm=512):
    t = u.shape[0]
    rows = t // SSM_CHUNK
    lanes = SSM_GB * SSM_GROUP
    blk = pl.BlockSpec((rows // bsz, SSM_CHUNK, lanes), lambda g, b: (b, 0, g))
    tab = lambda a: (pl.BlockSpec((1,) + a.shape[1:], lambda g, b: (g,) + (0,) * (a.ndim - 1)) if a.ndim > 2
                     else pl.BlockSpec(a.shape, lambda g, b: (0, 0)))
    width = SSM_GB * 2 * SSM_STATE
    state = pltpu.VMEM((rows // bsz, width), F32)
    y = pl.pallas_call(
        _s5_body,
        grid=(SSM_GROUPS // SSM_GB, bsz),
        in_specs=[blk] + [tab(a) for a in tables],
        out_specs=blk,
        out_shape=jax.ShapeDtypeStruct((rows, SSM_CHUNK, D_SSM), F32),
        scratch_shapes=[state, state, pltpu.VMEM((SSM_CHUNK * lanes, SSM_CHUNK * lanes), BF16),
                        pltpu.VMEM((SSM_CHUNK * lanes, width), BF16), pltpu.VMEM((width, SSM_CHUNK * lanes), BF16)],
        compiler_params=_params("parallel", "arbitrary"),
        name="s5_scan",
    )(u.reshape(rows, SSM_CHUNK, D_SSM), *tables)
    tm = min(tm, t)
    prm = jnp.stack([d_skip.astype(F32), norm_g.astype(F32)])
    row = pl.BlockSpec((tm, D_SSM), lambda i: (i, 0))
    return pl.pallas_call(
        _s5_post_body,
        grid=(t // tm,),
        in_specs=[row, row, pl.BlockSpec((2, D_SSM), lambda i: (0, 0)),
                  pl.BlockSpec((D_SSM, D_SSM), lambda i: (0, 0))],
        out_specs=row,
        out_shape=jax.ShapeDtypeStruct((t, D_SSM), BF16),
        compiler_params=_params("parallel"),
        name="s5_post",
    )(y.reshape(t, D_SSM), u, prm, w_glu.astype(BF16))


def _att_tables(rel_bias):
    qi = np.arange(ATT_TQ)[:, None]
    kj = np.arange(ATT_WIN)[None, :]
    back = kj // CHUNK - qi // CHUNK
    valid = (back >= 0) & (back <= LEFT_CHUNKS)
    rb = rel_bias.astype(F32)
    nh = rb.shape[0]
    shift = LEFT_CHUNKS * CHUNK
    n_mid = ATT_WIN - (shift - MAX_REL)
    assert shift >= MAX_REL and n_mid <= 2 * MAX_REL + 1
    desc = rb[:, ::-1]
    far = lambda n: jnp.broadcast_to(desc[:, :1], (nh, n))
    g = jnp.concatenate([far(shift - MAX_REL), desc[:, :n_mid], far(ATT_TQ)], axis=1)
    return g[:, None, :], jnp.asarray(np.where(valid, 0.0, -1e30), F32)


def _att_body(q_ref, k_ref, v_ref, g_ref, m_ref, o_ref):
    scale = ATT_HEAD_DIM ** -0.5
    lead = ATT_WIN // ATT_TQ - 1
    rows = jnp.broadcast_to(g_ref[0], (ATT_TQ, g_ref.shape[2]))
    bias_full = pltpu.roll(rows, 0, 1, stride=1, stride_axis=0)[:, :ATT_WIN] + m_ref[...]
    for i in range(q_ref.shape[1] // ATT_TQ):
        q0 = i * ATT_TQ
        k0 = max(0, q0 - (ATT_WIN - ATT_TQ))
        nk = q0 + ATT_TQ - k0
        bias = bias_full if i >= lead else bias_full[:, ATT_WIN - nk:]
        s = lax.dot_general(q_ref[0, q0:q0 + ATT_TQ, :], k_ref[0, k0:k0 + nk, :], NT_DIMS,
                            preferred_element_type=F32) * scale + bias
        p = jnp.exp(s - jnp.max(s, axis=-1, keepdims=True))
        l = jnp.sum(p, axis=-1, keepdims=True)
        o_ref[0, q0:q0 + ATT_TQ, :] = jnp.dot(p.astype(BF16), v_ref[0, k0:k0 + nk, :],
                                              preferred_element_type=F32) / l


def attention_call(qkv, rel_bias, norm_g, bsz):
    t = qkv.shape[0]
    seq = t // bsz
    x = qkv.reshape(bsz, seq, 3 * D_ATT)
    g, mask = _att_tables(rel_bias)
    head = lambda c: pl.BlockSpec((1, seq, ATT_HEAD_DIM), lambda b, h: (b, 0, c * ATT_HEADS + h))
    o = pl.pallas_call(
        _att_body,
        grid=(bsz, ATT_HEADS),
        in_specs=[head(0), head(1), head(2), pl.BlockSpec((1, 1, g.shape[2]), lambda b, h: (h, 0, 0)),
                  pl.BlockSpec(mask.shape, lambda b, h: (0, 0))],
        out_specs=head(0),
        out_shape=jax.ShapeDtypeStruct((bsz, seq, D_ATT), F32),
        compiler_params=_params("parallel", "parallel"),
        name="chunk_attention",
    )(x, x, x, g, mask)
    return rmsnorm_call(o.reshape(t, D_ATT), norm_g, BF16)


def _top_rows(x, k):
    vals, where_ = [], []
    idx = lax.broadcasted_iota(jnp.int32, x.shape, 0).astype(F32)
    for _ in range(k):
        m = jnp.max(x, axis=0, keepdims=True)
        first = jnp.min(jnp.where(x == m, idx, float(x.shape[0])), axis=0, keepdims=True)
        x = jnp.where(idx == first, -jnp.inf, x)
        vals.append(m)
        where_.append(first)
    return vals, where_


def _peer_topk_body(q_ref, k1_ref, k2_ref, n1_ref, c1_ref, rk_ref, e2_ref):
    half = PEER_QDIM // 2
    k = PEER_TOPK
    for h in range(PEER_HEADS):
        base = h * PEER_QDIM
        s1 = lax.dot_general(k1_ref[...], q_ref[:, base:base + half], NT_DIMS, preferred_element_type=F32)
        s2 = lax.dot_general(k2_ref[...], q_ref[:, base + half:base + PEER_QDIM], NT_DIMS,
                             preferred_element_type=F32)
        v1, at1 = _top_rows(s1, k)
        v2, at2 = _top_rows(s2, k)
        v2 = jnp.concatenate(v2, axis=0)
        cand = jnp.concatenate([v1[a] + v2[:k // (a + 1)] for a in range(k)], axis=0)
        top, _ = _top_rows(cand, k)
        tau = top[k - 1]
        z = jnp.ones_like(tau)
        for t in top[1:]:
            z = z + jnp.exp(t - top[0])
        idx = lax.broadcasted_iota(jnp.int32, s1.shape, 0).astype(F32)
        count = jnp.zeros_like(s1)
        rank = jnp.full_like(s2, float(k))
        for a in range(k):
            n_a = jnp.sum(jnp.where(v1[a] + v2 >= tau, 1.0, 0.0), axis=0, keepdims=True)
            count = jnp.where(idx == at1[a], n_a, count)
            rank = jnp.where(idx == at2[a], float(a), rank)
        n1_ref[h] = count
        c1_ref[h] = jnp.exp(s1 - v1[0]) / z
        rk_ref[h] = rank.astype(rk_ref.dtype)
        e2_ref[h] = jnp.exp(s2 - v2[0:1]).astype(e2_ref.dtype)


def peer_topk_call(q, keys1, keys2, tm=256):
    t = q.shape[0]
    hk = (PEER_HEADS, PEER_KEYS, t)
    blk = pl.BlockSpec((PEER_HEADS, PEER_KEYS, tm), lambda i: (0, 0, i))
    return pl.pallas_call(
        _peer_topk_body,
        grid=(t // tm,),
        in_specs=[pl.BlockSpec((tm, PEER_HEADS * PEER_QDIM), lambda i: (i, 0)),
                  pl.BlockSpec((PEER_KEYS, PEER_QDIM // 2), lambda i: (0, 0)),
                  pl.BlockSpec((PEER_KEYS, PEER_QDIM // 2), lambda i: (0, 0))],
        out_specs=[blk, blk, blk, blk],
        out_shape=[jax.ShapeDtypeStruct(hk, F32), jax.ShapeDtypeStruct(hk, F32),
                   jax.ShapeDtypeStruct(hk, BF16), jax.ShapeDtypeStruct(hk, BF16)],
        compiler_params=_params("parallel"),
        name="peer_topk",
    )(q, keys1, keys2)


def _peer_gate_body(x_ref, u_ref, n1_ref, c1_ref, rk_ref, e2_ref, w_ref, *, te):
    j = pl.program_id(1)
    ht = lax.dot_general(u_ref[...].astype(BF16), x_ref[...], NT_DIMS, preferred_element_type=F32)
    for s in range(te // PEER_KEYS):
        i1 = j * (te // PEER_KEYS) + s
        gate = None
        for h in range(PEER_HEADS):
            count = n1_ref[h, pl.ds(i1, 1), :].astype(BF16)
            factor = c1_ref[h, pl.ds(i1, 1), :].astype(BF16)
            term = jnp.where(rk_ref[h] < count, e2_ref[h], jnp.zeros((), BF16)) * factor
            gate = term if gate is None else gate + term
        act = jax.nn.gelu(ht[s * PEER_KEYS:(s + 1) * PEER_KEYS], approximate=True).astype(BF16)
        w_ref[:, s * PEER_KEYS:(s + 1) * PEER_KEYS] = jnp.transpose(gate * act)


def peer_gate_call(hn, u, layer, n1, c1, rk, e2, tm=1024, te=256):
    t, d = hn.shape
    e = u.shape[1]
    tm = min(tm, t)
    hk = pl.BlockSpec((PEER_HEADS, PEER_KEYS, tm), lambda i, j: (0, 0, i))
    return pl.pallas_call(
        functools.partial(_peer_gate_body, te=te),
        grid=(t // tm, e // te),
        in_specs=[pl.BlockSpec((tm, d), lambda i, j: (i, 0)),
                  pl.BlockSpec((None, te, d), lambda i, j: (layer, j, 0)),
                  hk, hk, hk, hk],
        out_specs=pl.BlockSpec((tm, te), lambda i, j: (i, j)),
        out_shape=jax.ShapeDtypeStruct((t, e), BF16),
        compiler_params=_params("parallel", "arbitrary"),
        name="peer_gate",
    )(hn, u, n1, c1, rk, e2)


def _mm_acc_body(a_ref, w_ref, o_ref):
    @pl.when(pl.program_id(2) == 0)
    def _():
        o_ref[...] = jnp.zeros_like(o_ref)

    o_ref[...] += jnp.dot(a_ref[...], w_ref[...].astype(BF16), preferred_element_type=F32)


def mm_acc_call(a, w, layer, tm=2048, tn=1024, tk=1024, name="mm_acc"):
    m, k = a.shape
    n = w.shape[2]
    tm, tn, tk = min(tm, m), min(tn, n), min(tk, k)
    return pl.pallas_call(
        _mm_acc_body,
        grid=(m // tm, n // tn, k // tk),
        in_specs=[pl.BlockSpec((tm, tk), lambda i, j, l: (i, l)),
                  pl.BlockSpec((None, tk, tn), lambda i, j, l: (layer, l, j))],
        out_specs=pl.BlockSpec((tm, tn), lambda i, j, l: (i, j)),
        out_shape=jax.ShapeDtypeStruct((m, n), F32),
        compiler_params=_params("parallel", "parallel", "arbitrary"),
        name=name,
    )(a, w)


def peer_ffn_call(h, norm_g, w_q, layer, keys1, keys2, u, v):
    hn = rmsnorm_call(h, norm_g, BF16)
    q = mm_call(hn, w_q, layer, 0, w_q.shape[-1], BF16, name="peer_q")
    n1, c1, rk, e2 = peer_topk_call(q, keys1.astype(BF16), keys2.astype(BF16))
    w = peer_gate_call(hn, u, layer, n1, c1, rk, e2)
    return mm_acc_call(w, v, layer, name="peer_out")


def mixer_call(xn, h, w_in, w_out, layer, lb, hgrn_norm, s5_tables, ssm_d, ssm_w_glu, ssm_norm, rel_bias, att_norm,
               bsz):
    n_a = 4 * D_HGRN
    proj_a = mm_call(xn, w_in, layer, 0, n_a, F32, name="in_proj_a")
    proj_b = mm_call(xn, w_in, layer, n_a, D_SSM, F32, name="in_proj_b")
    proj_c = mm_call(xn, w_in, layer, n_a + D_SSM, 3 * D_ATT, BF16, name="in_proj_c")
    y_a = hgrn_call(proj_a, lb, hgrn_norm, bsz)
    y_b = s5_call(proj_b, s5_tables, ssm_d, ssm_w_glu, ssm_norm, bsz)
    y_c = attention_call(proj_c, rel_bias, att_norm, bsz)
    return mm_parts_call([y_a, y_b, y_c], w_out, layer, h, name="out_proj")


def kernel(x, w_in, w_out, hgrn_lb, hgrn_norm, ssm_a_re, ssm_a_im, ssm_log_dt, ssm_b_re, ssm_b_im, ssm_c_re, ssm_c_im, ssm_d, ssm_w_glu, ssm_norm, att_rel_bias, att_norm, norm_mix, norm_ffn, peer_w_q, peer_keys1, peer_keys2, peer_u, peer_v, norm_final):
    bsz, seq, d = x.shape
    lb_all = jnp.cumsum(jax.nn.softmax(hgrn_lb.astype(F32), axis=0), axis=0)
    lb_all = lb_all - lb_all[0]
    h = x.reshape(bsz * seq, d)
    xn = rmsnorm_call(h, norm_mix[0], BF16)
    out = None
    for layer in range(DEPTH):
        tables = _s5_prep(ssm_a_re[layer], ssm_a_im[layer], ssm_log_dt[layer], ssm_b_re[layer],
                          ssm_b_im[layer], ssm_c_re[layer], ssm_c_im[layer])
        h = mixer_call(xn, h, w_in, w_out, layer, lb_all[layer], hgrn_norm[layer], tables,
                       ssm_d[layer], ssm_w_glu[layer], ssm_norm[layer], att_rel_bias[layer],
                       att_norm[layer], bsz)
        ffn = peer_ffn_call(h, norm_ffn[layer], peer_w_q, layer, peer_keys1[layer], peer_keys2[layer],
                            peer_u, peer_v)
        if layer + 1 < DEPTH:
            h, xn = add_rmsnorm_call(h, ffn, norm_mix[layer + 1], BF16, keep_sum=True)
        else:
            out = add_rmsnorm_call(h, ffn, norm_final, F32, keep_sum=False)
    return out.reshape(bsz, seq, d)
```
